```python
import math, functools
import jax
import jax.numpy as jnp
from jax import lax
import numpy as np

D_MODEL = 1024
BATCH = 8
SEQ = 2048
DEPTH = 4
DEC_BATCH = 32
DEC_SEQ = 8
PAST_LEN = 8192
PAGE_SIZE = 128

MIX = D_MODEL
W_MLSTM = MIX // 2
H_MLSTM = 4
DV_MLSTM = W_MLSTM // H_MLSTM
DQK_MLSTM = DV_MLSTM // 2
W_DIFF = MIX - W_MLSTM
H_DIFF = 4
DV_DIFF = W_DIFF // H_DIFF
DQK_DIFF = DV_DIFF // 2
MLSTM_CHUNK = 64
Q_BLOCK = 128
N_EXPERTS = 32
TOP_K = 4
D_FF = D_MODEL
SWIGLU_LIMIT = 7.0
SWIGLU_ALPHA = 1.702
MOE_BLOCK = 128
MOE_SMALL_BLOCK = 8
NORM_EPS = 1e-6
PROJ_SIZES = (H_MLSTM * DQK_MLSTM, H_MLSTM * DQK_MLSTM, W_MLSTM, H_MLSTM, H_MLSTM, W_MLSTM,
              H_DIFF * 2 * DQK_DIFF, H_DIFF * 2 * DQK_DIFF, W_DIFF)
N_PROJ = sum(PROJ_SIZES)

kernel_name = "hymba_mlstm_diffattn_alibi_moe_adaln_step"


def _alibi_slopes():
    return jnp.asarray(2.0 ** (-8.0 * np.arange(1, H_DIFF + 1) / H_DIFF), jnp.float32)


def _rmsnorm(x, g):
    xf = x.astype(jnp.float32)
    y = xf * lax.rsqrt(jnp.mean(xf * xf, axis=-1, keepdims=True) + NORM_EPS)
    return (y * g.astype(jnp.float32)).astype(x.dtype)


def _split_proj(h, w_in, b_igate, b_fgate):
    B, T, _ = h.shape
    idx = [int(i) for i in np.cumsum(PROJ_SIZES)[:-1]]
    qm, km, vm, ig, fg, og, qd, kd, vd = jnp.split(h @ w_in, idx, axis=-1)
    qm = qm.reshape(B, T, H_MLSTM, DQK_MLSTM)
    km = km.reshape(B, T, H_MLSTM, DQK_MLSTM)
    vm = vm.reshape(B, T, H_MLSTM, DV_MLSTM)
    ig = ig.astype(jnp.float32) + b_igate.astype(jnp.float32)
    lf = jax.nn.log_sigmoid(fg.astype(jnp.float32) + b_fgate.astype(jnp.float32))
    og = jax.nn.sigmoid(og)
    qd = qd.reshape(B, T, H_DIFF, 2, DQK_DIFF)
    kd = kd.reshape(B, T, H_DIFF, 2 * DQK_DIFF)
    vd = vd.reshape(B, T, H_DIFF, DV_DIFF)
    return qm, km, vm, ig, lf, og, qd, kd, vd


def _mlstm_chunkwise(q, k, v, ig, lf, C0, n0, m0):
    B, T, H, dk = q.shape
    dv = v.shape[-1]
    L = math.gcd(T, MLSTM_CHUNK)
    nc = T // L

    def chunks(a):
        a = a.astype(jnp.float32).reshape((B, nc, L, H) + a.shape[3:])
        return jnp.moveaxis(a, (1, 3), (0, 2))

    qc, kc, vc = chunks(q), chunks(k) * dk ** -0.5, chunks(v)
    igc, lfc = chunks(ig), chunks(lf)
    causal = jnp.tril(jnp.ones((L, L), dtype=bool))

    def step(carry, xs):
        C, n, m = carry
        qt, kt, vt, it, ft = xs
        b = jnp.cumsum(ft, axis=-1)
        dmat = jnp.where(causal, b[..., :, None] - b[..., None, :] + it[..., None, :], -jnp.inf)
        inter = b + m[..., None]
        m_row = jnp.maximum(jnp.max(dmat, axis=-1), inter)
        w_intra = jnp.einsum('bhtd,bhsd->bhts', qt, kt) * jnp.exp(dmat - m_row[..., None])
        w_inter = jnp.exp(inter - m_row)
        num = (jnp.einsum('bhts,bhsv->bhtv', w_intra, vt)
               + w_inter[..., None] * jnp.einsum('bhvd,bhtd->bhtv', C, qt))
        den = jnp.sum(w_intra, axis=-1) + w_inter * jnp.einsum('bhd,bhtd->bht', n, qt)
        h = num / jnp.maximum(jnp.abs(den), jnp.exp(-m_row))[..., None]
        b_end = b[..., -1]
        w_end = b_end[..., None] - b + it
        m_new = jnp.maximum(b_end + m, jnp.max(w_end, axis=-1))
        decay = jnp.exp(b_end + m - m_new)
        w_upd = jnp.exp(w_end - m_new[..., None])
        C = decay[..., None, None] * C + jnp.einsum('bhs,bhsv,bhsd->bhvd', w_upd, vt, kt)
        n = decay[..., None] * n + jnp.einsum('bhs,bhsd->bhd', w_upd, kt)
        return (C, n, m_new), h

    (C, n, m), hs = lax.scan(step, (C0, n0, m0), (qc, kc, vc, igc, lfc))
    h = jnp.moveaxis(hs, (0, 2), (1, 3)).reshape(B, T, H, dv)
    return h, C, n, m


def _diff_combine(s, lam):
    p = jax.nn.softmax(s, axis=-1)
    return p[:, :, 0] - lam * p[:, :, 1]


def _diff_attn_prompt(q, k, v, lam, slopes):
    B, S, H, _, dk = q.shape
    dv = v.shape[-1]
    qb_len = math.gcd(S, Q_BLOCK)
    nb = S // qb_len
    qh = jnp.moveaxis(q * dk ** -0.5, 1, 3)
    kh = jnp.moveaxis(k.reshape(B, S, H, 2, dk), 1, 3)
    vh = jnp.moveaxis(v, 1, 2)
    q_blocks = jnp.moveaxis(qh.reshape(B, H, 2, nb, qb_len, dk), 3, 0)
    pos_k = jnp.arange(S)

    def one_block(args):
        qblk, i = args
        pos_q = i * qb_len + jnp.arange(qb_len)
        dist = (pos_q[:, None] - pos_k[None, :]).astype(jnp.float32)
        bias = jnp.where(dist >= 0, -slopes[:, None, None] * dist, -jnp.inf)
        s = jnp.einsum('bhjqd,bhjkd->bhjqk', qblk, kh).astype(jnp.float32) + bias[None, :, None]
        a = _diff_combine(s, lam)
        return jnp.einsum('bhqk,bhkd->bhqd', a.astype(v.dtype), vh)

    out = lax.map(one_block, (q_blocks, jnp.arange(nb)))
    return jnp.moveaxis(out, (0, 2), (1, 3)).reshape(B, S, H, dv)


def _diff_attn_sample(q, k_new, v_new, lam, slopes, k_cache, v_cache, page_table):
    Bd, T, H, _, dk = q.shape
    dv = v_new.shape[-1]
    past = page_table.shape[1] * k_cache.shape[1]
    k_past = k_cache[page_table].reshape(Bd, past, H, 2, dk)
    v_past = v_cache[page_table].reshape(Bd, past, H, dv)
    qs = q * dk ** -0.5
    pos_q = past + jnp.arange(T)
    d_past = (pos_q[:, None] - jnp.arange(past)[None, :]).astype(jnp.float32)
    d_new = (pos_q[:, None] - pos_q[None, :]).astype(jnp.float32)
    s_past = (jnp.einsum('bqhjd,bkhjd->bhjqk', qs, k_past).astype(jnp.float32)
              - (slopes[:, None, None] * d_past)[None, :, None])
    s_new = (jnp.einsum('bqhjd,bkhjd->bhjqk', qs, k_new.reshape(Bd, T, H, 2, dk)).astype(jnp.float32)
             + jnp.where(d_new >= 0, -slopes[:, None, None] * d_new, -jnp.inf)[None, :, None])
    a = _diff_combine(jnp.concatenate([s_past, s_new], axis=-1), lam).astype(v_new.dtype)
    return (jnp.einsum('bhqk,bkhd->bqhd', a[..., :past], v_past)
            + jnp.einsum('bhqk,bkhd->bqhd', a[..., past:], v_new))


def _moe(h, router_w, router_b, w_gu, b_gu, w_down, b_down):
    shape = h.shape
    x = h.reshape(-1, shape[-1])
    T = x.shape[0]
    logits = (x @ router_w + router_b).astype(jnp.float32)
    top_v, top_i = lax.top_k(logits, TOP_K)
    gate = jax.nn.softmax(top_v, axis=-1)
    n_assign = T * TOP_K
    G = MOE_BLOCK if n_assign >= N_EXPERTS * MOE_BLOCK else MOE_SMALL_BLOCK
    nb = -(-(n_assign + N_EXPERTS * (G - 1)) // G)
    flat_e = top_i.reshape(-1)
    flat_t = jnp.repeat(jnp.arange(T, dtype=jnp.int32), TOP_K)
    flat_g = gate.reshape(-1)
    order = jnp.argsort(flat_e)
    se = flat_e[order]
    counts = jnp.zeros((N_EXPERTS,), jnp.int32).at[flat_e].add(1)
    padded = (counts + G - 1) // G * G
    pad_end = jnp.cumsum(padded)
    pad_start = pad_end - padded
    srt_start = jnp.cumsum(counts) - counts
    dest = pad_start[se] + jnp.arange(n_assign) - srt_start[se]
    P = nb * G
    slot_t = jnp.zeros((P,), jnp.int32).at[dest].set(flat_t[order])
    slot_g = jnp.zeros((P,), jnp.float32).at[dest].set(flat_g[order])
    block_e = jnp.minimum(jnp.searchsorted(pad_end, jnp.arange(nb) * G, side='right'), N_EXPERTS - 1)
    xs = x[slot_t].reshape(nb, G, shape[-1])

    def expert_block(args):
        xb, e = args
        gu = xb @ w_gu[e] + b_gu[e]
        g = jnp.minimum(gu[:, ::2], SWIGLU_LIMIT)
        u = jnp.clip(gu[:, 1::2], -SWIGLU_LIMIT, SWIGLU_LIMIT)
        act = (u + 1.0) * (g * jax.nn.sigmoid(SWIGLU_ALPHA * g))
        return act @ w_down[e] + b_down[e]

    ys = lax.map(expert_block, (xs, block_e)).reshape(P, shape[-1])
    out = jnp.zeros_like(x).at[slot_t].add((ys * slot_g[:, None]).astype(x.dtype))
    return out.reshape(shape)


def _block(x, c, mlstm_state, attend, lam, lam_init, norm1_g, ada_w, ada_b, w_in, b_igate, b_fgate,
           mlstm_norm_g, diff_norm_g, w_out, norm2_g, router_w, router_b, w_gu, b_gu, w_down, b_down):
    B, T, _ = x.shape
    mod = jax.nn.silu(c) @ ada_w + ada_b
    sh1, sc1, g1, sh2, sc2, g2 = [m[:, None, :] for m in jnp.split(mod, 6, axis=-1)]
    h = _rmsnorm(x, norm1_g) * (1.0 + sc1) + sh1
    qm, km, vm, ig, lf, og, qd, kd, vd = _split_proj(h, w_in, b_igate, b_fgate)
    hm, C, n, m = _mlstm_chunkwise(qm, km, vm, ig, lf, *mlstm_state)
    hm = _rmsnorm(hm.astype(x.dtype), mlstm_norm_g.reshape(H_MLSTM, DV_MLSTM)).reshape(B, T, W_MLSTM) * og
    hd = attend(qd, kd, vd, lam)
    hd = _rmsnorm(hd, diff_norm_g).reshape(B, T, W_DIFF) * (1.0 - lam_init)
    x = x + g1 * (jnp.concatenate([hm, hd], axis=-1) @ w_out)
    h2 = _rmsnorm(x, norm2_g) * (1.0 + sc2) + sh2
    x = x + g2 * _moe(h2, router_w, router_b, w_gu, b_gu, w_down, b_down)
    return x, (kd, vd, C, n, m)


def setup_inputs(seed: int = 0) -> dict:
    key = jax.random.key(seed)
    ks = iter(jax.random.split(key, 40))
    f32 = jnp.float32
    n_pages = PAST_LEN // PAGE_SIZE
    n_used = DEC_BATCH * n_pages
    n_pool = n_used + max(1, n_used // 4)

    def nrm(shape, scale=1.0):
        return scale * jax.random.normal(next(ks), shape, f32)

    inp = {}
    inp['x_prompt'] = nrm((BATCH, SEQ, D_MODEL))
    inp['x_sample'] = nrm((DEC_BATCH, DEC_SEQ, D_MODEL))
    inp['c_prompt'] = nrm((BATCH, D_MODEL))
    inp['c_sample'] = nrm((DEC_BATCH, D_MODEL))
    inp['cache_k'] = nrm((DEPTH, n_pool, PAGE_SIZE, H_DIFF, 2 * DQK_DIFF))
    inp['cache_v'] = nrm((DEPTH, n_pool, PAGE_SIZE, H_DIFF, DV_DIFF))
    inp['state_C'] = nrm((DEPTH, DEC_BATCH, H_MLSTM, DV_MLSTM, DQK_MLSTM))
    inp['state_n'] = nrm((DEPTH, DEC_BATCH, H_MLSTM, DQK_MLSTM))
    inp['state_m'] = nrm((DEPTH, DEC_BATCH, H_MLSTM))
    inp['page_table'] = jax.random.permutation(next(ks), n_pool)[:n_used].reshape(DEC_BATCH, n_pages).astype(jnp.int32)
    inp['norm1_g'] = 1.0 + nrm((DEPTH, D_MODEL), 0.02)
    inp['ada_w'] = nrm((DEPTH, D_MODEL, 6 * D_MODEL), 0.5 * D_MODEL ** -0.5)
    inp['ada_b'] = nrm((DEPTH, 6 * D_MODEL), 0.01)
    inp['w_in'] = nrm((DEPTH, D_MODEL, N_PROJ), D_MODEL ** -0.5)
    inp['b_igate'] = nrm((DEPTH, H_MLSTM), 0.1)
    inp['b_fgate'] = jnp.linspace(3.0, 6.0, H_MLSTM, dtype=f32)[None, :] + nrm((DEPTH, H_MLSTM), 0.1)
    inp['mlstm_norm_g'] = 1.0 + nrm((DEPTH, W_MLSTM), 0.02)
    inp['lam_q1'] = nrm((DEPTH, DQK_DIFF), 0.1)
    inp['lam_k1'] = nrm((DEPTH, DQK_DIFF), 0.1)
    inp['lam_q2'] = nrm((DEPTH, DQK_DIFF), 0.1)
    inp['lam_k2'] = nrm((DEPTH, DQK_DIFF), 0.1)
    inp['diff_norm_g'] = 1.0 + nrm((DEPTH, DV_DIFF), 0.02)
    inp['w_out'] = nrm((DEPTH, MIX, D_MODEL), MIX ** -0.5)
    inp['norm2_g'] = 1.0 + nrm((DEPTH, D_MODEL), 0.02)
    inp['router_w'] = nrm((DEPTH, D_MODEL, N_EXPERTS), D_MODEL ** -0.5)
    inp['router_b'] = nrm((DEPTH, N_EXPERTS), 0.01)
    inp['w_gu'] = nrm((DEPTH, N_EXPERTS, D_MODEL, 2 * D_FF), D_MODEL ** -0.5)
    inp['b_gu'] = nrm((DEPTH, N_EXPERTS, 2 * D_FF), 0.01)
    inp['w_down'] = nrm((DEPTH, N_EXPERTS, D_FF, D_MODEL), D_FF ** -0.5)
    inp['b_down'] = nrm((DEPTH, N_EXPERTS, D_MODEL), 0.01)
    inp['final_norm_g'] = 1.0 + nrm((D_MODEL,), 0.02)
    return inp


def reference(x_prompt, x_sample, c_prompt, c_sample, cache_k, cache_v, state_C, state_n, state_m,
              page_table, norm1_g, ada_w, ada_b, w_in, b_igate, b_fgate, mlstm_norm_g,
              lam_q1, lam_k1, lam_q2, lam_k2, diff_norm_g, w_out, norm2_g, router_w, router_b,
              w_gu, b_gu, w_down, b_down, final_norm_g):
    f32 = jnp.float32
    slopes = _alibi_slopes()
    xp, xs = x_prompt, x_sample
    Bp, Bd = x_prompt.shape[0], x_sample.shape[0]
    outs_p, outs_s = [], []
    for l in range(DEPTH):
        lam_init = 0.8 - 0.6 * math.exp(-0.3 * l)
        lam = (jnp.exp(jnp.sum(lam_q1[l].astype(f32) * lam_k1[l].astype(f32)))
               - jnp.exp(jnp.sum(lam_q2[l].astype(f32) * lam_k2[l].astype(f32))) + lam_init)
        weights = (norm1_g[l], ada_w[l], ada_b[l], w_in[l], b_igate[l], b_fgate[l], mlstm_norm_g[l],
                   diff_norm_g[l], w_out[l], norm2_g[l], router_w[l], router_b[l],
                   w_gu[l], b_gu[l], w_down[l], b_down[l])
        init_p = (jnp.zeros((Bp, H_MLSTM, DV_MLSTM, DQK_MLSTM), f32),
                  jnp.zeros((Bp, H_MLSTM, DQK_MLSTM), f32),
                  jnp.zeros((Bp, H_MLSTM), f32))
        attend_p = functools.partial(_diff_attn_prompt, slopes=slopes)
        xp, st_p = _block(xp, c_prompt, init_p, attend_p, lam, lam_init, *weights)
        init_s = (state_C[l].astype(f32), state_n[l].astype(f32), state_m[l].astype(f32))
        attend_s = functools.partial(_diff_attn_sample, slopes=slopes, k_cache=cache_k[l],
                                     v_cache=cache_v[l], page_table=page_table)
        xs, st_s = _block(xs, c_sample, init_s, attend_s, lam, lam_init, *weights)
        outs_p.append(st_p)
        outs_s.append(st_s)
    sdt = state_C.dtype
    y_prompt = _rmsnorm(xp, final_norm_g)
    y_sample = _rmsnorm(xs, final_norm_g)
    k_prompt = jnp.stack([o[0] for o in outs_p]).astype(cache_k.dtype)
    v_prompt = jnp.stack([o[1] for o in outs_p]).astype(cache_v.dtype)
    C_prompt = jnp.stack([o[2] for o in outs_p]).astype(sdt)
    n_prompt = jnp.stack([o[3] for o in outs_p]).astype(sdt)
    m_prompt = jnp.stack([o[4] for o in outs_p]).astype(sdt)
    k_sample = jnp.stack([o[0] for o in outs_s]).astype(cache_k.dtype)
    v_sample = jnp.stack([o[1] for o in outs_s]).astype(cache_v.dtype)
    C_sample = jnp.stack([o[2] for o in outs_s]).astype(sdt)
    n_sample = jnp.stack([o[3] for o in outs_s]).astype(sdt)
    m_sample = jnp.stack([o[4] for o in outs_s]).astype(sdt)
    return (y_prompt, y_sample, k_prompt, v_prompt, C_prompt, n_prompt, m_prompt,
            k_sample, v_sample, C_sample, n_sample, m_sample)
```

```python
import functools
import math

import numpy as np
import jax
import jax.numpy as jnp
from jax import lax
from jax.experimental import pallas as pl
from jax.experimental.pallas import tpu as pltpu

F32 = jnp.float32
BF16 = jnp.bfloat16
I32 = jnp.int32

NORM_EPS = 1e-6
N_HEADS = 4
HEAD_V = 128
HEAD_QK = 64
GROUP_W = N_HEADS * HEAD_V
TOP_K = 4
SWIGLU_LIMIT = 7.0
SWIGLU_ALPHA = 1.702
NEG_BIG = -1e30

ROW_TILE = 256
MLSTM_CHUNK = 256
MLSTM_PAD = 128
ATTN_TILE = 256
PAGES_PER_STEP = 8
VMEM_LIMIT = 48 * 1024 * 1024


def _cparams(n_axes, vmem=VMEM_LIMIT):
    return pltpu.CompilerParams(dimension_semantics=("arbitrary",) * n_axes, vmem_limit_bytes=vmem)


def _nt(a, b):
    return lax.dot_general(a, b, (((1,), (1,)), ((), ())), preferred_element_type=F32)


def _mm(a, b):
    return jnp.dot(a, b, preferred_element_type=F32)


def _split3(x):
    hi = x.astype(BF16)
    r1 = x - hi.astype(F32)
    mid = r1.astype(BF16)
    lo = (r1 - mid.astype(F32)).astype(BF16)
    return hi, mid, lo


def _log_sigmoid(x):
    return jnp.minimum(x, 0.0) - jnp.log1p(jnp.exp(-jnp.abs(x)))


def _rms(x, g):
    return x * lax.rsqrt(jnp.mean(x * x, axis=-1, keepdims=True) + NORM_EPS) * g


def _mod_kernel(c_ref, w_ref, b_ref, o_ref):
    c = c_ref[...]
    s = c * jax.nn.sigmoid(c)
    o_ref[...] = _mm(s.astype(BF16), w_ref[...].astype(BF16)) + b_ref[...]


def _adaln_mod(c_all, ada_w, ada_b):
    n_layers, d, n = ada_w.shape
    bc = c_all.shape[0]
    tn = n // 4
    return pl.pallas_call(
        _mod_kernel,
        grid=(n_layers, n // tn),
        in_specs=[pl.BlockSpec((bc, d), lambda l, j: (0, 0)),
                  pl.BlockSpec((None, d, tn), lambda l, j: (l, 0, j)),
                  pl.BlockSpec((None, 1, tn), lambda l, j: (l, 0, j))],
        out_specs=pl.BlockSpec((None, bc, tn), lambda l, j: (l, 0, j)),
        out_shape=jax.ShapeDtypeStruct((n_layers, bc, n), F32),
        compiler_params=_cparams(2),
        name="adaln_mod",
    )(c_all, ada_w, ada_b.reshape(n_layers, 1, n))


def _mod_spec(per_row, tm, d, rows_per_batch):
    if per_row:
        return pl.BlockSpec((6, tm, d), lambda i: (0, i, 0))
    tiles_per_batch = rows_per_batch // tm
    return pl.BlockSpec((None, 6, 1, d), lambda i: (i // tiles_per_batch, 0, 0, 0))


def _inproj_kernel(x_ref, mod_ref, g_ref, wa_ref, wb_ref, wg_ref, wgt_ref, bgc_ref, bgr_ref,
                   qk_ref, vm_ref, og_ref, qd_ref, kd_ref, vd_ref, gcol_ref, grow_ref):
    h = (_rms(x_ref[...], g_ref[...]) * (1.0 + mod_ref[1]) + mod_ref[0]).astype(BF16)
    a = _mm(h, wa_ref[...])
    lane = lax.broadcasted_iota(I32, (1, GROUP_W), 1)
    k_scale = jnp.where(lane >= N_HEADS * HEAD_QK, HEAD_QK ** -0.5, 1.0)
    qk_ref[...] = (a[:, :GROUP_W] * k_scale).astype(BF16)
    vm_ref[...] = a[:, GROUP_W:].astype(BF16)
    b = _mm(h, wb_ref[...])
    og_ref[...] = jax.nn.sigmoid(b[:, :GROUP_W]).astype(BF16)
    qd_ref[...] = (b[:, GROUP_W:2 * GROUP_W] * HEAD_QK ** -0.5).astype(BF16)
    kd_ref[...] = b[:, 2 * GROUP_W:3 * GROUP_W]
    vd_ref[...] = b[:, 3 * GROUP_W:]
    gc = _mm(h, wg_ref[...]) + bgc_ref[...]
    lane_g = lax.broadcasted_iota(I32, gc.shape, 1)
    gcol_ref[...] = jnp.where((lane_g >= N_HEADS) & (lane_g < 2 * N_HEADS), _log_sigmoid(gc), gc)
    gr = _nt(wgt_ref[...], h) + bgr_ref[...]
    row_g = lax.broadcasted_iota(I32, gr.shape, 0)
    grow_ref[...] = jnp.where(row_g >= N_HEADS, _log_sigmoid(gr), gr)


def _inproj(l, x, mod, per_row, rows_per_batch, norm_g, wa, wb, wg, wgt, bgc, bgr):
    t, d = x.shape
    tm = ROW_TILE
    nt = t // tm
    row = lambda w, dt: jax.ShapeDtypeStruct((t, w), dt)
    rspec = lambda w: pl.BlockSpec((tm, w), lambda i: (i, 0))
    wspec = lambda a: pl.BlockSpec((None,) + a.shape[1:], lambda i: (l,) + (0,) * (a.ndim - 1))
    return pl.pallas_call(
        _inproj_kernel,
        grid=(nt,),
        in_specs=[rspec(d), _mod_spec(per_row, tm, d, rows_per_batch), wspec(norm_g),
                  wspec(wa), wspec(wb), wspec(wg), wspec(wgt), wspec(bgc), wspec(bgr)],
        out_specs=[rspec(GROUP_W)] * 6 + [rspec(128), pl.BlockSpec((None, 2 * N_HEADS, tm), lambda i: (i, 0, 0))],
        out_shape=[row(GROUP_W, BF16)] * 4 + [row(GROUP_W, F32)] * 2
                  + [row(128, F32), jax.ShapeDtypeStruct((nt, 2 * N_HEADS, tm), F32)],
        compiler_params=_cparams(1),
        name="norm_inproj",
    )(x, mod, norm_g, wa, wb, wg, wgt, bgc, bgr)


def _mlstm_kernel(qk_ref, v_ref, og_ref, gc_ref, gr_ref, c0_ref, n0_ref, m0_ref, g_ref,
                  h_ref, c_ref, n_ref, m_ref, *, chunk):
    @pl.when(pl.program_id(1) == 0)
    def _():
        c_ref[...] = c0_ref[...]
        n_ref[...] = n0_ref[...]
        m_ref[...] = m0_ref[...]

    gc = gc_ref[...]
    gr = gr_ref[...]
    r = lax.broadcasted_iota(I32, (chunk, chunk), 0)
    s = lax.broadcasted_iota(I32, (chunk, chunk), 1)
    causal = r >= s
    tri = causal.astype(BF16)
    tri_t = (r <= s).astype(BF16)
    lane_g = lax.broadcasted_iota(I32, gc.shape, 1)
    gl = jnp.where(lane_g >= N_HEADS, gc, 0.0)
    cum_c = sum(_mm(tri, p) for p in _split3(gl))
    cum_r = sum(_mm(p, tri_t) for p in _split3(gr))
    lane = lax.broadcasted_iota(I32, (1, HEAD_V), 1)

    for h in range(N_HEADS):
        t0 = (h // 2) * HEAD_V
        off = (h % 2) * HEAD_QK
        hmask = (lane >= off) & (lane < off + HEAD_QK)
        zero = jnp.zeros((), BF16)
        qh = jnp.where(hmask, qk_ref[:, t0:t0 + HEAD_V], zero)
        kh = jnp.where(hmask, qk_ref[:, N_HEADS * HEAD_QK + t0:N_HEADS * HEAD_QK + t0 + HEAD_V], zero)
        hs = slice(h * HEAD_V, (h + 1) * HEAD_V)
        vh = v_ref[:, hs]
        ig_r = gr[h:h + 1, :]
        b_r = cum_r[N_HEADS + h:N_HEADS + h + 1, :]
        ig_c = gc[:, h:h + 1]
        b_c = cum_c[:, N_HEADS + h:N_HEADS + h + 1]
        m_prev = m_ref[h:h + 1, 0:1]
        c_prev = c_ref[h]
        n_prev = n_ref[h:h + 1, :]

        dmat = jnp.where(causal, b_c - b_r + ig_r, -jnp.inf)
        inter = b_c + m_prev
        m_row = jnp.maximum(jnp.max(dmat, axis=1, keepdims=True), inter)
        w_intra = _nt(qh, kh) * jnp.exp(dmat - m_row)
        w_inter = jnp.exp(inter - m_row)
        num = _mm(w_intra.astype(BF16), vh) + w_inter * _nt(qh, c_prev.astype(BF16))
        den = (jnp.sum(w_intra, axis=1, keepdims=True)
               + w_inter * jnp.sum(qh.astype(F32) * n_prev, axis=1, keepdims=True))
        hv = num / jnp.maximum(jnp.abs(den), jnp.exp(-m_row))
        h_ref[:, hs] = (_rms(hv, g_ref[:, hs]) * og_ref[:, hs].astype(F32)).astype(BF16)

        b_end = b_c[chunk - 1:chunk, :]
        w_end = b_end - b_r + ig_r
        m_new = jnp.maximum(b_end + m_prev, jnp.max(w_end, axis=1, keepdims=True))
        decay = jnp.exp(b_end + m_prev - m_new)
        kw = kh.astype(F32) * jnp.exp(b_end - b_c + ig_c - m_new)
        v_t = vh.astype(F32).T.astype(BF16)
        c_ref[h] = decay * c_prev + _mm(v_t, kw.astype(BF16))
        n_ref[h:h + 1, :] = decay * n_prev + jnp.sum(kw, axis=0, keepdims=True)
        m_ref[h:h + 1, :] = jnp.broadcast_to(m_new, (1, HEAD_V))


def _mlstm(l, qk, vm, og, gcol, grow, c0, n0, m0, norm_g, n_seq, n_chunks, chunk):
    t = qk.shape[0]
    rspec = lambda w: pl.BlockSpec((chunk, w), lambda b, c: (b * n_chunks + c, 0))
    sspec = lambda shp: pl.BlockSpec((None,) + shp, lambda b, c: (b,) + (0,) * len(shp))
    c_shape, s_shape = (N_HEADS, HEAD_V, HEAD_V), (2 * N_HEADS, HEAD_V)
    return pl.pallas_call(
        functools.partial(_mlstm_kernel, chunk=chunk),
        grid=(n_seq, n_chunks),
        in_specs=[rspec(GROUP_W), rspec(GROUP_W), rspec(GROUP_W), rspec(128),
                  pl.BlockSpec((None, 2 * N_HEADS, chunk), lambda b, c: (b * n_chunks + c, 0, 0)),
                  sspec(c_shape), sspec(s_shape), sspec(s_shape),
                  pl.BlockSpec((None, 1, GROUP_W), lambda b, c: (l, 0, 0))],
        out_specs=[rspec(GROUP_W), sspec(c_shape), sspec(s_shape), sspec(s_shape)],
        out_shape=[jax.ShapeDtypeStruct((t, GROUP_W), BF16),
                   jax.ShapeDtypeStruct((n_seq,) + c_shape, F32),
                   jax.ShapeDtypeStruct((n_seq,) + s_shape, F32),
                   jax.ShapeDtypeStruct((n_seq,) + s_shape, F32)],
        compiler_params=_cparams(2),
        name="mlstm",
    )(qk, vm, og, gcol, grow, c0, n0, m0, norm_g)


def _pad_state(c, n, m):
    b = c.shape[0]
    c_pad = jnp.zeros((b, N_HEADS, HEAD_V, HEAD_V), F32)
    n_pad = jnp.zeros((b, 2 * N_HEADS, HEAD_V), F32)
    for h in range(N_HEADS):
        off = (h % 2) * HEAD_QK
        c_pad = c_pad.at[:, h, :, off:off + HEAD_QK].set(c[:, h].astype(F32))
        n_pad = n_pad.at[:, h, off:off + HEAD_QK].set(n[:, h].astype(F32))
    m_pad = jnp.zeros((b, 2 * N_HEADS, HEAD_V), F32).at[:, :N_HEADS, :].set(
        jnp.broadcast_to(m.astype(F32)[:, :, None], (b, N_HEADS, HEAD_V)))
    return c_pad, n_pad, m_pad


def _unpad_state(c_pad, n_pad, m_pad):
    c = jnp.stack([c_pad[:, h, :, (h % 2) * HEAD_QK:(h % 2) * HEAD_QK + HEAD_QK] for h in range(N_HEADS)], 1)
    n = jnp.stack([n_pad[:, h, (h % 2) * HEAD_QK:(h % 2) * HEAD_QK + HEAD_QK] for h in range(N_HEADS)], 1)
    return c, n, m_pad[:, :N_HEADS, 0]


def _head_finish(o1, o2, lam, g, out_scale):
    d = o1 - lam * o2
    return _rms(d, g) * out_scale


def _attn_prompt_kernel(sc_ref, q_ref, k_ref, v_ref, g_ref, o_ref, kb_ref, vb_ref, *, tile, out_scale):
    h = pl.program_id(1)
    qi = pl.program_id(2)

    @pl.when(qi == 0)
    def _():
        kb_ref[...] = k_ref[...].astype(BF16)
        vb_ref[...] = v_ref[...].astype(BF16)

    lam = sc_ref[0]
    slope = sc_ref[1 + h]
    q = q_ref[...]
    lane = lax.broadcasted_iota(I32, (1, HEAD_V), 1)
    zero = jnp.zeros((), BF16)
    q2 = jnp.concatenate([jnp.where(lane < HEAD_QK, q, zero), jnp.where(lane >= HEAD_QK, q, zero)], axis=0)
    col = lax.broadcasted_iota(I32, (1, tile), 1)
    row = lax.broadcasted_iota(I32, (2 * tile, 1), 0)
    row = jnp.where(row >= tile, row - tile, row)

    def step(j, carry, diagonal):
        m, lsum, acc = carry
        start = pl.multiple_of(j * tile, tile)
        ks = kb_ref[pl.ds(start, tile), :]
        vs = vb_ref[pl.ds(start, tile), :]
        rel = col + (j - qi) * tile
        sc = _nt(q2, ks) + slope * rel.astype(F32)
        if diagonal:
            sc = jnp.where(col <= row, sc, -jnp.inf)
        m_new = jnp.maximum(m, jnp.max(sc, axis=1, keepdims=True))
        alpha = jnp.exp(m - m_new)
        p = jnp.exp(sc - m_new)
        lsum = alpha * lsum + jnp.sum(p, axis=1, keepdims=True)
        acc = alpha * acc + _mm(p.astype(BF16), vs)
        return m_new, lsum, acc

    init = (jnp.full((2 * tile, 1), -jnp.inf, F32), jnp.zeros((2 * tile, 1), F32),
            jnp.zeros((2 * tile, HEAD_V), F32))
    carry = lax.fori_loop(0, qi, lambda j, c: step(j, c, False), init)
    _, lsum, acc = step(qi, carry, True)
    o = acc / lsum
    o_ref[...] = _head_finish(o[:tile], o[tile:], lam, g_ref[...], out_scale).astype(BF16)


def _attn_prompt(l, scalars, qd, kd, vd, norm_g, n_seq, seq, out_scale):
    t = qd.shape[0]
    tile = min(ATTN_TILE, seq)
    nq = seq // tile
    return pl.pallas_call(
        functools.partial(_attn_prompt_kernel, tile=tile, out_scale=out_scale),
        grid=(n_seq, N_HEADS, nq),
        in_specs=[pl.BlockSpec(memory_space=pltpu.SMEM),
                  pl.BlockSpec((tile, HEAD_V), lambda b, h, i: (b * nq + i, h)),
                  pl.BlockSpec((seq, HEAD_V), lambda b, h, i: (b, h)),
                  pl.BlockSpec((seq, HEAD_V), lambda b, h, i: (b, h)),
                  pl.BlockSpec((None, 1, HEAD_V), lambda b, h, i: (l, 0, 0))],
        out_specs=pl.BlockSpec((tile, HEAD_V), lambda b, h, i: (b * nq + i, h)),
        out_shape=jax.ShapeDtypeStruct((t, GROUP_W), BF16),
        scratch_shapes=[pltpu.VMEM((seq, HEAD_V), BF16), pltpu.VMEM((seq, HEAD_V), BF16)],
        compiler_params=_cparams(3),
        name="attn_prompt",
    )(scalars, qd, kd, vd, norm_g)


def _attn_paged_kernel(pt_ref, sc_ref, wt_ref, *refs, n_pages, past, out_scale):
    del pt_ref
    k_refs, v_refs = refs[:n_pages], refs[n_pages:2 * n_pages]
    kn_ref, vn_ref, ci_ref, g_ref, o_ref, m_ref, l_ref, acc_ref = refs[2 * n_pages:]
    s = pl.program_id(1)
    n_keys = n_pages * k_refs[0].shape[0]

    @pl.when(s == 0)
    def _():
        m_ref[...] = jnp.full(m_ref.shape, -jnp.inf, F32)
        l_ref[...] = jnp.zeros(l_ref.shape, F32)
        acc_ref[...] = jnp.zeros(acc_ref.shape, F32)

    wt = wt_ref[...]
    slope = ci_ref[:, 0:1]
    tok = ci_ref[:, 1:2]

    def update(sc, v_bf):
        m_old = m_ref[...]
        m_new = jnp.maximum(m_old, jnp.max(sc, axis=1, keepdims=True))
        alpha = jnp.exp(m_old - m_new)
        p = jnp.exp(sc - m_new)
        l_ref[...] = alpha * l_ref[...] + jnp.sum(p, axis=1, keepdims=True)
        acc_ref[...] = alpha * acc_ref[...] + _mm(p.astype(BF16), v_bf)
        m_ref[...] = m_new

    k_bf = jnp.concatenate([r[...] for r in k_refs], axis=0).astype(BF16)
    v_bf = jnp.concatenate([r[...] for r in v_refs], axis=0).astype(BF16)
    kpos = lax.broadcasted_iota(I32, (1, n_keys), 1) + (s * n_keys - past)
    update(_nt(wt, k_bf) + slope * (kpos.astype(F32) - tok), v_bf)

    @pl.when(s == pl.num_programs(1) - 1)
    def _():
        r = lax.broadcasted_iota(I32, (1, kn_ref.shape[0]), 1).astype(F32)
        sc = _nt(wt, kn_ref[...].astype(BF16)) + slope * (r - tok)
        update(jnp.where(r <= tok, sc, -jnp.inf), vn_ref[...].astype(BF16))
        o = acc_ref[...] / l_ref[...]
        lam = sc_ref[0]
        n_tok = o_ref.shape[0]
        for h in range(N_HEADS):
            hs = slice(h * HEAD_V, (h + 1) * HEAD_V)
            c0 = h * 2 * n_tok
            o_ref[:, hs] = _head_finish(o[c0:c0 + n_tok, hs], o[c0 + n_tok:c0 + 2 * n_tok, hs],
                                        lam, g_ref[...], out_scale)


def _attn_paged(l, page_table, scalars, wt, cache_k, cache_v, k_new, v_new, cinfo, norm_g, out_scale):
    n_seq, pages_per_seq = page_table.shape
    page = cache_k.shape[2]
    n_pages = min(PAGES_PER_STEP, pages_per_seq)
    n_steps = pages_per_seq // n_pages
    n_tok = 8

    def page_spec(i):
        return pl.BlockSpec((None, None, page, GROUP_W),
                            lambda b, s, pt: (l, pt[b, s * n_pages + i], 0, 0))

    seq_spec = lambda shp: pl.BlockSpec((None,) + shp, lambda b, s, pt: (b,) + (0,) * len(shp))
    grid_spec = pltpu.PrefetchScalarGridSpec(
        num_scalar_prefetch=1,
        grid=(n_seq, n_steps),
        in_specs=[pl.BlockSpec(memory_space=pltpu.SMEM), seq_spec((128, GROUP_W))]
                 + [page_spec(i) for i in range(n_pages)] * 2
                 + [seq_spec((128, GROUP_W)), seq_spec((128, GROUP_W)),
                    pl.BlockSpec((128, 128), lambda b, s, pt: (0, 0)),
                    pl.BlockSpec((None, 1, HEAD_V), lambda b, s, pt: (l, 0, 0))],
        out_specs=seq_spec((n_tok, GROUP_W)),
        scratch_shapes=[pltpu.VMEM((128, 1), F32), pltpu.VMEM((128, 1), F32), pltpu.VMEM((128, GROUP_W), F32)],
    )
    return pl.pallas_call(
        functools.partial(_attn_paged_kernel, n_pages=n_pages, past=pages_per_seq * page, out_scale=out_scale),
        grid_spec=grid_spec,
        out_shape=jax.ShapeDtypeStruct((n_seq, n_tok, GROUP_W), F32),
        compiler_params=_cparams(2),
        name="attn_paged",
    )(page_table, scalars, wt, *([cache_k] * n_pages), *([cache_v] * n_pages), k_new, v_new, cinfo, norm_g)


def _paged_query_matrix(qd, n_seq, n_tok):
    q5 = qd.reshape(n_seq, n_tok, N_HEADS, 2, HEAD_QK).transpose(0, 2, 3, 1, 4)
    eye_h = jnp.eye(N_HEADS, dtype=qd.dtype)
    eye_j = jnp.eye(2, dtype=qd.dtype)
    wt = jnp.einsum('bhjtd,hg,ji->bhjtgid', q5, eye_h, eye_j)
    wt = wt.reshape(n_seq, N_HEADS * 2 * n_tok, GROUP_W)
    return jnp.pad(wt, ((0, 0), (0, 128 - wt.shape[1]), (0, 0)))


def _paged_row_info(n_tok):
    info = np.zeros((128, 128), np.float32)
    for h in range(N_HEADS):
        for j in range(2):
            for tkn in range(n_tok):
                c = h * 2 * n_tok + j * n_tok + tkn
                info[c, 0] = 2.0 ** (-8.0 * (h + 1) / N_HEADS)
                info[c, 1] = tkn
    return jnp.asarray(info)


def _outproj_kernel(x_ref, hm_ref, hd_ref, mod_ref, g_ref, w_ref, rwh_ref, rwl_ref, rb_ref,
                    xo_ref, h2_ref, ti_ref, tg_ref):
    mix = jnp.concatenate([hm_ref[...], hd_ref[...]], axis=1)
    xn = x_ref[...] + mod_ref[2] * _mm(mix, w_ref[...])
    xo_ref[...] = xn
    h2 = _rms(xn, g_ref[...]) * (1.0 + mod_ref[4]) + mod_ref[3]
    h2_ref[...] = h2
    hh = h2.astype(BF16)
    hl = (h2 - hh.astype(F32)).astype(BF16)
    lg = _nt(rwh_ref[...], hh) + _nt(rwh_ref[...], hl) + _nt(rwl_ref[...], hh) + rb_ref[...]
    n_exp = lg.shape[0]
    eidx = lax.broadcasted_iota(I32, lg.shape, 0).astype(F32)
    vals, ids = [], []
    for _ in range(TOP_K):
        mx = jnp.max(lg, axis=0, keepdims=True)
        ik = jnp.min(jnp.where(lg == mx, eidx, float(n_exp)), axis=0, keepdims=True)
        vals.append(mx)
        ids.append(ik)
        lg = jnp.where(eidx == ik, -jnp.inf, lg)
    e = [jnp.exp(v - vals[0]) for v in vals]
    tot = e[0] + e[1] + e[2] + e[3]
    ti_ref[...] = jnp.concatenate(ids, axis=0).astype(I32)
    tg_ref[...] = jnp.concatenate(e, axis=0) / tot


def _outproj(l, x, hm, hd, mod, per_row, rows_per_batch, norm_g, w_out, rw_hi, rw_lo, rb):
    t, d = x.shape
    tm = ROW_TILE
    rspec = lambda w: pl.BlockSpec((tm, w), lambda i: (i, 0))
    wspec = lambda a: pl.BlockSpec((None,) + a.shape[1:], lambda i: (l,) + (0,) * (a.ndim - 1))
    tspec = pl.BlockSpec((TOP_K, tm), lambda i: (0, i))
    return pl.pallas_call(
        _outproj_kernel,
        grid=(t // tm,),
        in_specs=[rspec(d), rspec(GROUP_W), rspec(GROUP_W), _mod_spec(per_row, tm, d, rows_per_batch),
                  wspec(norm_g), wspec(w_out), wspec(rw_hi), wspec(rw_lo), wspec(rb)],
        out_specs=[rspec(d), rspec(d), tspec, tspec],
        out_shape=[jax.ShapeDtypeStruct((t, d), F32), jax.ShapeDtypeStruct((t, d), F32),
                   jax.ShapeDtypeStruct((TOP_K, t), I32), jax.ShapeDtypeStruct((TOP_K, t), F32)],
        compiler_params=_cparams(1),
        name="outproj_norm_router",
    )(x, hm, hd, mod, norm_g, w_out, rw_hi, rw_lo, rb)


def _route(top_i, tm, n_exp):
    t = top_i.shape[1]
    n_assign = t * TOP_K
    flat_e = top_i.T.reshape(-1)
    onehot = (flat_e[:, None] == jnp.arange(n_exp, dtype=I32)[None, :]).astype(I32)
    csum = jnp.cumsum(onehot, axis=0)
    counts = csum[-1]
    rank = jnp.sum(csum * onehot, axis=1) - 1
    padded = (counts + tm - 1) // tm * tm
    pad_end = jnp.cumsum(padded)
    pad_start = pad_end - padded
    pos = (jnp.sum(pad_start[None, :] * onehot, axis=1) + rank).astype(I32)
    n_blocks = -(-(n_assign + n_exp * (tm - 1)) // tm)
    slot_t = jnp.zeros((n_blocks * tm,), I32).at[pos].set(
        jnp.arange(n_assign, dtype=I32) // TOP_K, unique_indices=True)
    block_e = jnp.minimum(jnp.searchsorted(pad_end, jnp.arange(n_blocks, dtype=I32) * tm, side='right'),
                          n_exp - 1).astype(I32)
    n_used = (pad_end[-1:] // tm).astype(I32)
    return pos, slot_t, block_e, n_used, n_blocks


def _experts_kernel(slot_ref, be_ref, nu_ref, x_hbm, wg_ref, wu_ref, wd_ref, bg_ref, bu_ref, bd_ref,
                    y_ref, xbuf, sem, *, tm):
    del be_ref
    i = pl.program_id(0)
    n_used = nu_ref[0]

    def gather(blk, slot):
        def body(r, carry):
            tok = slot_ref[blk * tm + r]
            pltpu.make_async_copy(x_hbm.at[pl.ds(tok, 1), :], xbuf.at[slot, pl.ds(r, 1), :], sem.at[slot]).start()
            return carry
        lax.fori_loop(0, tm, body, 0)

    @pl.when(i == 0)
    def _():
        gather(0, 0)

    slot = i % 2

    @pl.when(i + 1 < n_used)
    def _():
        gather(i + 1, 1 - slot)

    @pl.when(i < n_used)
    def _():
        pltpu.make_async_copy(x_hbm.at[pl.ds(0, tm), :], xbuf.at[slot], sem.at[slot]).wait()
        x = xbuf[slot].astype(BF16)
        g = jnp.minimum(_mm(x, wg_ref[...]) + bg_ref[...], SWIGLU_LIMIT)
        u = jnp.clip(_mm(x, wu_ref[...]) + bu_ref[...], -SWIGLU_LIMIT, SWIGLU_LIMIT)
        act = (u + 1.0) * (g * jax.nn.sigmoid(SWIGLU_ALPHA * g))
        y_ref[...] = _mm(act.astype(BF16), wd_ref[...]) + bd_ref[...]

    @pl.when(i >= n_used)
    def _():
        y_ref[...] = jnp.zeros(y_ref.shape, F32)


def _experts(l, slot_t, block_e, n_used, n_blocks, h2, wg, wu, wd, bg, bu, bd):
    tm = ROW_TILE
    d = h2.shape[1]
    wspec = lambda a: pl.BlockSpec((None, None) + a.shape[2:], lambda i, st, be, nu: (l, be[i]) + (0,) * (a.ndim - 2))
    grid_spec = pltpu.PrefetchScalarGridSpec(
        num_scalar_prefetch=3,
        grid=(n_blocks,),
        in_specs=[pl.BlockSpec(memory_space=pl.ANY), wspec(wg), wspec(wu), wspec(wd), wspec(bg), wspec(bu), wspec(bd)],
        out_specs=pl.BlockSpec((tm, d), lambda i, st, be, nu: (i, 0)),
        scratch_shapes=[pltpu.VMEM((2, tm, d), F32), pltpu.SemaphoreType.DMA((2,))],
    )
    return pl.pallas_call(
        functools.partial(_experts_kernel, tm=tm),
        grid_spec=grid_spec,
        out_shape=jax.ShapeDtypeStruct((n_blocks * tm, d), F32),
        compiler_params=_cparams(1),
        name="experts",
    )(slot_t, block_e, n_used, h2, wg, wu, wd, bg, bu, bd)


def _combine_kernel(pos_ref, x_ref, mod_ref, gate_ref, y_hbm, fg_ref, o_ref, buf, sem, *, tm, final):
    i = pl.program_id(0)

    def gather(blk, slot):
        def body(r, carry):
            for k in range(TOP_K):
                p = pos_ref[(blk * tm + r) * TOP_K + k]
                pltpu.make_async_copy(y_hbm.at[pl.ds(p, 1), :], buf.at[slot, k, pl.ds(r, 1), :], sem.at[slot]).start()
            return carry
        lax.fori_loop(0, tm, body, 0)

    @pl.when(i == 0)
    def _():
        gather(0, 0)

    slot = i % 2

    @pl.when(i + 1 < pl.num_programs(0))
    def _():
        gather(i + 1, 1 - slot)

    for k in range(TOP_K):
        pltpu.make_async_copy(y_hbm.at[pl.ds(0, tm), :], buf.at[slot, k], sem.at[slot]).wait()
    gate = gate_ref[...]
    y = gate[:, 0:1] * buf[slot, 0]
    for k in range(1, TOP_K):
        y = y + gate[:, k:k + 1] * buf[slot, k]
    out = x_ref[...] + mod_ref[5] * y
    o_ref[...] = _rms(out, fg_ref[...]) if final else out


def _combine(pos, x, mod, per_row, rows_per_batch, gate, ys, final_g, final):
    t, d = x.shape
    tm = ROW_TILE
    grid_spec = pltpu.PrefetchScalarGridSpec(
        num_scalar_prefetch=1,
        grid=(t // tm,),
        in_specs=[pl.BlockSpec((tm, d), lambda i, p: (i, 0)),
                  _wrap_prefetch(_mod_spec(per_row, tm, d, rows_per_batch)),
                  pl.BlockSpec((tm, TOP_K), lambda i, p: (i, 0)),
                  pl.BlockSpec(memory_space=pl.ANY),
                  pl.BlockSpec((1, d), lambda i, p: (0, 0))],
        out_specs=pl.BlockSpec((tm, d), lambda i, p: (i, 0)),
        scratch_shapes=[pltpu.VMEM((2, TOP_K, tm, d), F32), pltpu.SemaphoreType.DMA((2,))],
    )
    return pl.pallas_call(
        functools.partial(_combine_kernel, tm=tm, final=final),
        grid_spec=grid_spec,
        out_shape=jax.ShapeDtypeStruct((t, d), F32),
        compiler_params=_cparams(1),
        name="combine",
    )(pos, x, mod, gate, ys, final_g)


def _wrap_prefetch(spec):
    return pl.BlockSpec(spec.block_shape, lambda i, p: spec.index_map(i))


def kernel(x_prompt, x_sample, c_prompt, c_sample, cache_k, cache_v, state_C, state_n, state_m, page_table, norm1_g, ada_w, ada_b, w_in, b_igate, b_fgate, mlstm_norm_g, lam_q1, lam_k1, lam_q2, lam_k2, diff_norm_g, w_out, norm2_g, router_w, router_b, w_gu, b_gu, w_down, b_down, final_norm_g):
    n_layers = ada_w.shape[0]
    bp, seq, d = x_prompt.shape
    bd, n_tok, _ = x_sample.shape
    n_exp = router_w.shape[-1]
    tp, ts = bp * seq, bd * n_tok
    nqk = N_HEADS * HEAD_QK
    assert d == 2 * GROUP_W and n_tok == 8 and seq % MLSTM_CHUNK == 0 and ts % ROW_TILE == 0
    assert w_in.shape[-1] == 2 * nqk + GROUP_W + 2 * N_HEADS + 4 * GROUP_W

    g_lo = 2 * nqk + GROUP_W
    wa = w_in[:, :, :g_lo].astype(BF16)
    wb = w_in[:, :, g_lo + 2 * N_HEADS:].astype(BF16)
    w_gate = w_in[:, :, g_lo:g_lo + 2 * N_HEADS]
    wg_col = jnp.pad(w_gate, ((0, 0), (0, 0), (0, 128 - 2 * N_HEADS))).astype(BF16)
    wg_row = jnp.swapaxes(w_gate, 1, 2).astype(BF16)
    b_gate = jnp.concatenate([b_igate, b_fgate], axis=1).astype(F32)
    bg_col = jnp.pad(b_gate, ((0, 0), (0, 128 - 2 * N_HEADS)))[:, None, :]
    bg_row = b_gate[:, :, None]
    w_out_bf = w_out.astype(BF16)
    rw_t = jnp.swapaxes(router_w, 1, 2).astype(F32)
    rw_hi = rw_t.astype(BF16)
    rw_lo = (rw_t - rw_hi.astype(F32)).astype(BF16)
    rb = router_b.astype(F32)[:, :, None]
    w_g = w_gu[..., 0::2].astype(BF16)
    w_u = w_gu[..., 1::2].astype(BF16)
    w_d = w_down.astype(BF16)
    b_g = b_gu[..., None, 0::2].astype(F32)
    b_u = b_gu[..., None, 1::2].astype(F32)
    b_d = b_down[..., None, :].astype(F32)
    n1g, n2g = norm1_g[:, None, :], norm2_g[:, None, :]
    mng, dng = mlstm_norm_g[:, None, :], diff_norm_g[:, None, :]
    fng = final_norm_g[None, :]
    slopes = jnp.asarray(2.0 ** (-8.0 * np.arange(1, N_HEADS + 1) / N_HEADS), F32)
    pool, page = cache_k.shape[1], cache_k.shape[2]
    ck = cache_k.reshape(n_layers, pool, page, GROUP_W)
    cv = cache_v.reshape(n_layers, pool, page, GROUP_W)
    cinfo = _paged_row_info(n_tok)

    mod_all = _adaln_mod(jnp.concatenate([c_prompt, c_sample], axis=0), ada_w, ada_b)

    xp = x_prompt.reshape(tp, d)
    xs = x_sample.reshape(ts, d)
    zero_state = _pad_state(jnp.zeros((bp, N_HEADS, HEAD_V, HEAD_QK), F32), jnp.zeros((bp, N_HEADS, HEAD_QK), F32),
                            jnp.zeros((bp, N_HEADS), F32))
    n_chunks = seq // MLSTM_CHUNK
    outs = []
    for l in range(n_layers):
        lam_init = 0.8 - 0.6 * math.exp(-0.3 * l)
        lam = (jnp.exp(jnp.sum(lam_q1[l].astype(F32) * lam_k1[l].astype(F32)))
               - jnp.exp(jnp.sum(lam_q2[l].astype(F32) * lam_k2[l].astype(F32))) + lam_init)
        scalars = jnp.concatenate([lam[None], slopes]).astype(F32)
        mod_p = mod_all[l, :bp].reshape(bp, 6, 1, d)
        mod_s = jnp.repeat(mod_all[l, bp:].reshape(bd, 6, d).transpose(1, 0, 2), n_tok, axis=1)
        proj_w = (n1g, wa, wb, wg_col, wg_row, bg_col, bg_row)

        qk, vm, og, qd, kd_p, vd_p, gcol, grow = _inproj(l, xp, mod_p, False, seq, *proj_w)
        grow = grow.reshape(tp // ROW_TILE, 2 * N_HEADS, ROW_TILE // MLSTM_CHUNK, MLSTM_CHUNK)
        grow = grow.transpose(0, 2, 1, 3).reshape(tp // MLSTM_CHUNK, 2 * N_HEADS, MLSTM_CHUNK)
        hm_p, c_p, n_p, m_p = _mlstm(l, qk, vm, og, gcol, grow, *zero_state, mng, bp, n_chunks, MLSTM_CHUNK)
        hd_p = _attn_prompt(l, scalars, qd, kd_p, vd_p, dng, bp, seq, 1.0 - lam_init)

        qk, vm, og, qd, kd_s, vd_s, gcol, grow = _inproj(l, xs, mod_s, True, 0, *proj_w)
        pad_rows = lambda a: jnp.pad(a.reshape(bd, n_tok, a.shape[-1]),
                                     ((0, 0), (0, MLSTM_PAD - n_tok), (0, 0)))
        gcol = gcol.reshape(bd, n_tok, 128)
        gcol = jnp.concatenate(
            [gcol, jnp.broadcast_to(jnp.where(jnp.arange(128) < N_HEADS, NEG_BIG, 0.0).astype(F32),
                                    (bd, MLSTM_PAD - n_tok, 128))], axis=1).reshape(bd * MLSTM_PAD, 128)
        grow = grow.transpose(1, 0, 2).reshape(2 * N_HEADS, bd, n_tok).transpose(1, 0, 2)
        grow = jnp.concatenate(
            [grow, jnp.broadcast_to(jnp.where(jnp.arange(2 * N_HEADS) < N_HEADS, NEG_BIG, 0.0).astype(F32)[None, :, None],
                                    (bd, 2 * N_HEADS, MLSTM_PAD - n_tok))], axis=2)
        flat = lambda a: pad_rows(a).reshape(bd * MLSTM_PAD, a.shape[-1])
        hm_s, c_s, n_s, m_s = _mlstm(l, flat(qk), flat(vm), flat(og), gcol, grow,
                                     *_pad_state(state_C[l], state_n[l], state_m[l]), mng, bd, 1, MLSTM_PAD)
        hm_s = hm_s.reshape(bd, MLSTM_PAD, GROUP_W)[:, :n_tok].reshape(ts, GROUP_W)
        wt = _paged_query_matrix(qd, bd, n_tok)
        hd_s = _attn_paged(l, page_table, scalars, wt, ck, cv, pad_rows(kd_s), pad_rows(vd_s), cinfo, dng,
                           1.0 - lam_init)
        hd_s = hd_s.reshape(ts, GROUP_W).astype(BF16)

        post_w = (n2g, w_out_bf, rw_hi, rw_lo, rb)
        xp, h2_p, ti_p, tg_p = _outproj(l, xp, hm_p, hd_p, mod_p, False, seq, *post_w)
        xs, h2_s, ti_s, tg_s = _outproj(l, xs, hm_s, hd_s, mod_s, True, 0, *post_w)
        h2 = jnp.concatenate([h2_p, h2_s], axis=0)
        pos, slot_t, block_e, n_used, n_blocks = _route(jnp.concatenate([ti_p, ti_s], axis=1), ROW_TILE, n_exp)
        ys = _experts(l, slot_t, block_e, n_used, n_blocks, h2, w_g, w_u, w_d, b_g, b_u, b_d)
        final = l == n_layers - 1
        xp = _combine(pos[:tp * TOP_K], xp, mod_p, False, seq, tg_p.T, ys, fng, final)
        xs = _combine(pos[tp * TOP_K:], xs, mod_s, True, 0, tg_s.T, ys, fng, final)

        outs.append((kd_p, vd_p, c_p, n_p, m_p, kd_s, vd_s, c_s, n_s, m_s))

    def stacked(i, shape, dtype):
        return jnp.stack([o[i] for o in outs]).reshape((n_layers,) + shape).astype(dtype)

    states_p = [_unpad_state(o[2], o[3], o[4]) for o in outs]
    states_s = [_unpad_state(o[7], o[8], o[9]) for o in outs]
    sdt = state_C.dtype
    st = lambda states, i: jnp.stack([s[i] for s in states]).astype(sdt)
    return (xp.reshape(bp, seq, d), xs.reshape(bd, n_tok, d),
            stacked(0, (bp, seq, N_HEADS, 2 * HEAD_QK), cache_k.dtype),
            stacked(1, (bp, seq, N_HEADS, HEAD_V), cache_v.dtype),
            st(states_p, 0), st(states_p, 1), st(states_p, 2),
            stacked(5, (bd, n_tok, N_HEADS, 2 * HEAD_QK), cache_k.dtype),
            stacked(6, (bd, n_tok, N_HEADS, HEAD_V), cache_v.dtype),
            st(states_s, 0), st(states_s, 1), st(states_s, 2))
```

```python
import functools
import math

import numpy as np
import jax
import jax.numpy as jnp
from jax import lax
from jax.experimental import pallas as pl
from jax.experimental.pallas import tpu as pltpu

F32 = jnp.float32
BF16 = jnp.bfloat16
I32 = jnp.int32

NORM_EPS = 1e-6
N_HEADS = 4
HEAD_V = 128
HEAD_QK = 64
GROUP_W = N_HEADS * HEAD_V
TOP_K = 4
SWIGLU_LIMIT = 7.0
SWIGLU_ALPHA = 1.702
NEG_BIG = -1e30

ROW_TILE = 256
MLSTM_CHUNK = 256
MLSTM_PAD = 128
ATTN_TILE = 256
ATTN_CHUNK = 128
PAGES_PER_STEP = 8
VMEM_LIMIT = 48 * 1024 * 1024


def _cparams(n_axes, vmem=VMEM_LIMIT):
    return pltpu.CompilerParams(dimension_semantics=("arbitrary",) * n_axes, vmem_limit_bytes=vmem)


def _nt(a, b):
    return lax.dot_general(a, b, (((1,), (1,)), ((), ())), preferred_element_type=F32)


def _mm(a, b):
    return jnp.dot(a, b, preferred_element_type=F32)


def _split3(x):
    hi = x.astype(BF16)
    r1 = x - hi.astype(F32)
    mid = r1.astype(BF16)
    lo = (r1 - mid.astype(F32)).astype(BF16)
    return hi, mid, lo


def _log_sigmoid(x):
    return jnp.minimum(x, 0.0) - jnp.log1p(jnp.exp(-jnp.abs(x)))


def _rms(x, g):
    return x * lax.rsqrt(jnp.mean(x * x, axis=-1, keepdims=True) + NORM_EPS) * g


ROW_CHUNKS = 8


def _to_row_tiles(ref, x):
    rows, c_n = x.shape[0], x.shape[1] // 128
    for c in range(c_n):
        ref[pl.ds(c, rows, stride=c_n), :] = x[:, c * 128:(c + 1) * 128]


def _from_row_tiles(ref, c_n=ROW_CHUNKS):
    rows = ref.shape[0] // c_n
    return jnp.concatenate([ref[pl.ds(c, rows, stride=c_n), :] for c in range(c_n)], axis=1)


def _mod_kernel(c_ref, w_ref, b_ref, o_ref):
    c = c_ref[...]
    s = c * jax.nn.sigmoid(c)
    o_ref[...] = _mm(s.astype(BF16), w_ref[...].astype(BF16)) + b_ref[...]


def _adaln_mod(c_all, ada_w, ada_b):
    n_layers, d, n = ada_w.shape
    bc = c_all.shape[0]
    tn = n // 4
    return pl.pallas_call(
        _mod_kernel,
        grid=(n_layers, n // tn),
        in_specs=[pl.BlockSpec((bc, d), lambda l, j: (0, 0)),
                  pl.BlockSpec((None, d, tn), lambda l, j: (l, 0, j)),
                  pl.BlockSpec((None, 1, tn), lambda l, j: (l, 0, j))],
        out_specs=pl.BlockSpec((None, bc, tn), lambda l, j: (l, 0, j)),
        out_shape=jax.ShapeDtypeStruct((n_layers, bc, n), F32),
        compiler_params=_cparams(2),
        name="adaln_mod",
    )(c_all, ada_w, ada_b.reshape(n_layers, 1, n))


def _mod_spec(per_row, tm, d, rows_per_batch):
    if per_row:
        return pl.BlockSpec((6, tm, d), lambda i: (0, i, 0))
    tiles_per_batch = rows_per_batch // tm
    return pl.BlockSpec((None, 6, 1, d), lambda i: (i // tiles_per_batch, 0, 0, 0))


def _inproj_kernel(x_ref, mod_ref, g_ref, wa_ref, wb_ref, wg_ref, wgt_ref, bgc_ref, bgr_ref,
                   qk_ref, vm_ref, og_ref, qd_ref, kd_ref, vd_ref, gcol_ref, grow_ref):
    h = (_rms(x_ref[...], g_ref[...]) * (1.0 + mod_ref[1]) + mod_ref[0]).astype(BF16)
    a = _mm(h, wa_ref[...])
    lane = lax.broadcasted_iota(I32, (1, GROUP_W), 1)
    k_scale = jnp.where(lane >= N_HEADS * HEAD_QK, HEAD_QK ** -0.5, 1.0)
    qk_ref[...] = (a[:, :GROUP_W] * k_scale).astype(BF16)
    vm_ref[...] = a[:, GROUP_W:].astype(BF16)
    b = _mm(h, wb_ref[...])
    og_ref[...] = jax.nn.sigmoid(b[:, :GROUP_W]).astype(BF16)
    qd_ref[...] = (b[:, GROUP_W:2 * GROUP_W] * HEAD_QK ** -0.5).astype(BF16)
    kd_ref[...] = b[:, 2 * GROUP_W:3 * GROUP_W]
    vd_ref[...] = b[:, 3 * GROUP_W:]
    gc = _mm(h, wg_ref[...]) + bgc_ref[...]
    lane_g = lax.broadcasted_iota(I32, gc.shape, 1)
    gcol_ref[...] = jnp.where((lane_g >= N_HEADS) & (lane_g < 2 * N_HEADS), _log_sigmoid(gc), gc)
    gr = _nt(wgt_ref[...], h) + bgr_ref[...]
    row_g = lax.broadcasted_iota(I32, gr.shape, 0)
    grow_ref[...] = jnp.where(row_g >= N_HEADS, _log_sigmoid(gr), gr)


def _inproj(l, x, mod, per_row, rows_per_batch, norm_g, wa, wb, wg, wgt, bgc, bgr):
    t, d = x.shape
    tm = ROW_TILE
    nt = t // tm
    row = lambda w, dt: jax.ShapeDtypeStruct((t, w), dt)
    rspec = lambda w: pl.BlockSpec((tm, w), lambda i: (i, 0))
    wspec = lambda a: pl.BlockSpec((None,) + a.shape[1:], lambda i: (l,) + (0,) * (a.ndim - 1))
    return pl.pallas_call(
        _inproj_kernel,
        grid=(nt,),
        in_specs=[rspec(d), _mod_spec(per_row, tm, d, rows_per_batch), wspec(norm_g),
                  wspec(wa), wspec(wb), wspec(wg), wspec(wgt), wspec(bgc), wspec(bgr)],
        out_specs=[rspec(GROUP_W)] * 6 + [rspec(128), pl.BlockSpec((None, 2 * N_HEADS, tm), lambda i: (i, 0, 0))],
        out_shape=[row(GROUP_W, BF16)] * 4 + [row(GROUP_W, F32)] * 2
                  + [row(128, F32), jax.ShapeDtypeStruct((nt, 2 * N_HEADS, tm), F32)],
        compiler_params=_cparams(1),
        name="norm_inproj",
    )(x, mod, norm_g, wa, wb, wg, wgt, bgc, bgr)


def _mlstm_kernel(qk_ref, v_ref, og_ref, gc_ref, gr_ref, c0_ref, n0_ref, m0_ref, g_ref,
                  h_ref, c_ref, n_ref, m_ref, *, chunk):
    @pl.when(pl.program_id(1) == 0)
    def _():
        c_ref[...] = c0_ref[...]
        n_ref[...] = n0_ref[...]
        m_ref[...] = m0_ref[...]

    gc = gc_ref[...]
    gr = gr_ref[...]
    r = lax.broadcasted_iota(I32, (chunk, chunk), 0)
    s = lax.broadcasted_iota(I32, (chunk, chunk), 1)
    causal = r >= s
    tri = causal.astype(BF16)
    tri_t = (r <= s).astype(BF16)
    lane_g = lax.broadcasted_iota(I32, gc.shape, 1)
    gl = jnp.where(lane_g >= N_HEADS, gc, 0.0)
    cum_c = sum(_mm(tri, p) for p in _split3(gl))
    cum_r = sum(_mm(p, tri_t) for p in _split3(gr))
    lane = lax.broadcasted_iota(I32, (1, HEAD_V), 1)

    for h in range(N_HEADS):
        t0 = (h // 2) * HEAD_V
        off = (h % 2) * HEAD_QK
        hmask = (lane >= off) & (lane < off + HEAD_QK)
        zero = jnp.zeros((), BF16)
        qh = jnp.where(hmask, qk_ref[:, t0:t0 + HEAD_V], zero)
        kh = jnp.where(hmask, qk_ref[:, N_HEADS * HEAD_QK + t0:N_HEADS * HEAD_QK + t0 + HEAD_V], zero)
        hs = slice(h * HEAD_V, (h + 1) * HEAD_V)
        vh = v_ref[:, hs]
        ig_r = gr[h:h + 1, :]
        b_r = cum_r[N_HEADS + h:N_HEADS + h + 1, :]
        ig_c = gc[:, h:h + 1]
        b_c = cum_c[:, N_HEADS + h:N_HEADS + h + 1]
        m_prev = m_ref[h:h + 1, 0:1]
        c_prev = c_ref[h]
        n_prev = n_ref[h:h + 1, :]

        dmat = jnp.where(causal, b_c - b_r + ig_r, -jnp.inf)
        inter = b_c + m_prev
        m_row = jnp.maximum(jnp.max(dmat, axis=1, keepdims=True), inter)
        w_intra = _nt(qh, kh) * jnp.exp(dmat - m_row)
        w_inter = jnp.exp(inter - m_row)
        num = _mm(w_intra.astype(BF16), vh) + w_inter * _nt(qh, c_prev.astype(BF16))
        den = (jnp.sum(w_intra, axis=1, keepdims=True)
               + w_inter * jnp.sum(qh.astype(F32) * n_prev, axis=1, keepdims=True))
        hv = num / jnp.maximum(jnp.abs(den), jnp.exp(-m_row))
        h_ref[:, hs] = (_rms(hv, g_ref[:, hs]) * og_ref[:, hs].astype(F32)).astype(BF16)

        b_end = b_c[chunk - 1:chunk, :]
        w_end = b_end - b_r + ig_r
        m_new = jnp.maximum(b_end + m_prev, jnp.max(w_end, axis=1, keepdims=True))
        decay = jnp.exp(b_end + m_prev - m_new)
        kw = kh.astype(F32) * jnp.exp(b_end - b_c + ig_c - m_new)
        v_t = vh.astype(F32).T.astype(BF16)
        c_ref[h] = decay * c_prev + _mm(v_t, kw.astype(BF16))
        n_ref[h:h + 1, :] = decay * n_prev + jnp.sum(kw, axis=0, keepdims=True)
        m_ref[h:h + 1, :] = jnp.broadcast_to(m_new, (1, HEAD_V))


def _mlstm(l, qk, vm, og, gcol, grow, c0, n0, m0, norm_g, n_seq, n_chunks, chunk):
    t = qk.shape[0]
    rspec = lambda w: pl.BlockSpec((chunk, w), lambda b, c: (b * n_chunks + c, 0))
    sspec = lambda shp: pl.BlockSpec((None,) + shp, lambda b, c: (b,) + (0,) * len(shp))
    c_shape, s_shape = (N_HEADS, HEAD_V, HEAD_V), (2 * N_HEADS, HEAD_V)
    return pl.pallas_call(
        functools.partial(_mlstm_kernel, chunk=chunk),
        grid=(n_seq, n_chunks),
        in_specs=[rspec(GROUP_W), rspec(GROUP_W), rspec(GROUP_W), rspec(128),
                  pl.BlockSpec((None, 2 * N_HEADS, chunk), lambda b, c: (b * n_chunks + c, 0, 0)),
                  sspec(c_shape), sspec(s_shape), sspec(s_shape),
                  pl.BlockSpec((None, 1, GROUP_W), lambda b, c: (l, 0, 0))],
        out_specs=[rspec(GROUP_W), sspec(c_shape), sspec(s_shape), sspec(s_shape)],
        out_shape=[jax.ShapeDtypeStruct((t, GROUP_W), BF16),
                   jax.ShapeDtypeStruct((n_seq,) + c_shape, F32),
                   jax.ShapeDtypeStruct((n_seq,) + s_shape, F32),
                   jax.ShapeDtypeStruct((n_seq,) + s_shape, F32)],
        compiler_params=_cparams(2),
        name="mlstm",
    )(qk, vm, og, gcol, grow, c0, n0, m0, norm_g)


def _pad_state(c, n, m):
    b = c.shape[0]
    c_pad = jnp.zeros((b, N_HEADS, HEAD_V, HEAD_V), F32)
    n_pad = jnp.zeros((b, 2 * N_HEADS, HEAD_V), F32)
    for h in range(N_HEADS):
        off = (h % 2) * HEAD_QK
        c_pad = c_pad.at[:, h, :, off:off + HEAD_QK].set(c[:, h].astype(F32))
        n_pad = n_pad.at[:, h, off:off + HEAD_QK].set(n[:, h].astype(F32))
    m_pad = jnp.zeros((b, 2 * N_HEADS, HEAD_V), F32).at[:, :N_HEADS, :].set(
        jnp.broadcast_to(m.astype(F32)[:, :, None], (b, N_HEADS, HEAD_V)))
    return c_pad, n_pad, m_pad


def _unpad_state(c_pad, n_pad, m_pad):
    c = jnp.stack([c_pad[:, h, :, (h % 2) * HEAD_QK:(h % 2) * HEAD_QK + HEAD_QK] for h in range(N_HEADS)], 1)
    n = jnp.stack([n_pad[:, h, (h % 2) * HEAD_QK:(h % 2) * HEAD_QK + HEAD_QK] for h in range(N_HEADS)], 1)
    return c, n, m_pad[:, :N_HEADS, 0]


def _head_finish(o1, o2, lam, g, out_scale):
    d = o1 - lam * o2
    return _rms(d, g) * out_scale


def _attn_prompt_kernel(sc_ref, q_ref, k_ref, v_ref, g_ref, o_ref, kb_ref, vb_ref, q2_ref, *state,
                        tile, chunk, out_scale):
    h = pl.program_id(1)
    qi = pl.program_id(2)
    n_chunks = 2 * tile // chunk
    m_refs, l_refs, acc_refs = state[0::3], state[1::3], state[2::3]

    @pl.when(qi == 0)
    def _():
        kb_ref[...] = k_ref[...].astype(BF16)
        vb_ref[...] = v_ref[...].astype(BF16)

    lam = sc_ref[0]
    slope = sc_ref[1 + h]
    q = q_ref[...]
    lane = lax.broadcasted_iota(I32, (1, HEAD_V), 1)
    zero = jnp.zeros((), BF16)
    q2_ref[0:tile, :] = jnp.where(lane < HEAD_QK, q, zero)
    q2_ref[tile:2 * tile, :] = jnp.where(lane >= HEAD_QK, q, zero)
    for c in range(n_chunks):
        m_refs[c][...] = jnp.full((chunk, HEAD_V), -jnp.inf, F32)
        l_refs[c][...] = jnp.zeros((chunk, HEAD_V), F32)
        acc_refs[c][...] = jnp.zeros((chunk, HEAD_V), F32)
    col = lax.broadcasted_iota(I32, (1, tile), 1)
    ones = jnp.ones((tile, HEAD_V), BF16)

    def qk(j):
        ks = kb_ref[pl.ds(pl.multiple_of(j * tile, tile), tile), :]
        return tuple(_nt(q2_ref[c * chunk:(c + 1) * chunk, :], ks) for c in range(n_chunks))

    def step(j, scores, diagonal):
        v_ones = jnp.concatenate([vb_ref[pl.ds(pl.multiple_of(j * tile, tile), tile), :], ones], axis=1)
        bias = slope * (col + (j - qi) * tile).astype(F32)
        for c in range(n_chunks):
            sc = scores[c] + bias
            if diagonal:
                row = lax.broadcasted_iota(I32, (chunk, 1), 0) + (c * chunk) % tile
                sc = jnp.where(col <= row, sc, -jnp.inf)
            m_old = m_refs[c][...]
            m_new = jnp.maximum(m_old, jnp.max(sc, axis=1, keepdims=True))
            alpha = jnp.exp(m_old - m_new)
            p = jnp.exp(sc - jnp.concatenate([m_new] * (tile // HEAD_V), axis=1)).astype(BF16)
            pv = _mm(p, v_ones)
            acc_refs[c][...] = alpha * acc_refs[c][...] + pv[:, :HEAD_V]
            l_refs[c][...] = alpha * l_refs[c][...] + pv[:, HEAD_V:]
            m_refs[c][...] = m_new

    def body(j, scores):
        nxt = qk(j + 1)
        step(j, scores, False)
        return nxt

    scores = lax.fori_loop(0, qi, body, qk(0))
    step(qi, scores, True)
    o = jnp.concatenate([acc_refs[c][...] / l_refs[c][...] for c in range(n_chunks)], axis=0)
    o_ref[...] = _head_finish(o[:tile], o[tile:], lam, g_ref[...], out_scale).astype(BF16)


def _attn_prompt(l, scalars, qd, kd, vd, norm_g, n_seq, seq, out_scale):
    t = qd.shape[0]
    tile = min(ATTN_TILE, seq)
    chunk = min(ATTN_CHUNK, tile)
    nq = seq // tile
    return pl.pallas_call(
        functools.partial(_attn_prompt_kernel, tile=tile, chunk=chunk, out_scale=out_scale),
        grid=(n_seq, N_HEADS, nq),
        in_specs=[pl.BlockSpec(memory_space=pltpu.SMEM),
                  pl.BlockSpec((tile, HEAD_V), lambda b, h, i: (b * nq + i, h)),
                  pl.BlockSpec((seq, HEAD_V), lambda b, h, i: (b, h)),
                  pl.BlockSpec((seq, HEAD_V), lambda b, h, i: (b, h)),
                  pl.BlockSpec((None, 1, HEAD_V), lambda b, h, i: (l, 0, 0))],
        out_specs=pl.BlockSpec((tile, HEAD_V), lambda b, h, i: (b * nq + i, h)),
        out_shape=jax.ShapeDtypeStruct((t, GROUP_W), BF16),
        scratch_shapes=[pltpu.VMEM((seq, HEAD_V), BF16), pltpu.VMEM((seq, HEAD_V), BF16),
                        pltpu.VMEM((2 * tile, HEAD_V), BF16)]
                       + [pltpu.VMEM((chunk, HEAD_V), F32)] * (3 * (2 * tile // chunk)),
        compiler_params=_cparams(3),
        name="attn_prompt",
    )(scalars, qd, kd, vd, norm_g)


def _attn_paged_kernel(pt_ref, sc_ref, wt_ref, *refs, n_pages, past, out_scale):
    del pt_ref
    k_refs, v_refs = refs[:n_pages], refs[n_pages:2 * n_pages]
    kn_ref, vn_ref, ci_ref, g_ref, o_ref, m_ref, l_ref, acc_ref = refs[2 * n_pages:]
    s = pl.program_id(1)
    n_keys = n_pages * k_refs[0].shape[0] // N_HEADS

    @pl.when(s == 0)
    def _():
        m_ref[...] = jnp.full(m_ref.shape, -jnp.inf, F32)
        l_ref[...] = jnp.zeros(l_ref.shape, F32)
        acc_ref[...] = jnp.zeros(acc_ref.shape, F32)

    wt = wt_ref[...]
    slope = ci_ref[:, 0:1]
    tok = ci_ref[:, 1:2]

    def update(sc, v_bf):
        m_old = m_ref[...]
        m_new = jnp.maximum(m_old, jnp.max(sc, axis=1, keepdims=True))
        alpha = jnp.exp(m_old - m_new)
        p = jnp.exp(sc - m_new)
        l_ref[...] = alpha * l_ref[...] + jnp.sum(p, axis=1, keepdims=True)
        acc_ref[...] = alpha * acc_ref[...] + _mm(p.astype(BF16), v_bf)
        m_ref[...] = m_new

    def pages(page_refs):
        return jnp.concatenate([_from_row_tiles(r, N_HEADS) for r in page_refs], axis=0).astype(BF16)

    k_bf = pages(k_refs)
    v_bf = pages(v_refs)
    kpos = lax.broadcasted_iota(I32, (1, n_keys), 1) + (s * n_keys - past)
    update(_nt(wt, k_bf) + slope * (kpos.astype(F32) - tok), v_bf)

    @pl.when(s == pl.num_programs(1) - 1)
    def _():
        r = lax.broadcasted_iota(I32, (1, kn_ref.shape[0]), 1).astype(F32)
        sc = _nt(wt, kn_ref[...].astype(BF16)) + slope * (r - tok)
        update(jnp.where(r <= tok, sc, -jnp.inf), vn_ref[...].astype(BF16))
        o = acc_ref[...] / l_ref[...]
        lam = sc_ref[0]
        n_tok = o_ref.shape[0]
        for h in range(N_HEADS):
            hs = slice(h * HEAD_V, (h + 1) * HEAD_V)
            c0 = h * 2 * n_tok
            o_ref[:, hs] = _head_finish(o[c0:c0 + n_tok, hs], o[c0 + n_tok:c0 + 2 * n_tok, hs],
                                        lam, g_ref[...], out_scale)


def _attn_paged(l, page_table, scalars, wt, cache_k, cache_v, k_new, v_new, cinfo, norm_g, out_scale):
    n_seq, pages_per_seq = page_table.shape
    page = cache_k.shape[2] // N_HEADS
    n_pages = min(PAGES_PER_STEP, pages_per_seq)
    n_steps = pages_per_seq // n_pages
    n_tok = 8

    def page_spec(i):
        return pl.BlockSpec((None, None, page * N_HEADS, HEAD_V),
                            lambda b, s, pt: (l, pt[b, s * n_pages + i], 0, 0))

    seq_spec = lambda shp: pl.BlockSpec((None,) + shp, lambda b, s, pt: (b,) + (0,) * len(shp))
    grid_spec = pltpu.PrefetchScalarGridSpec(
        num_scalar_prefetch=1,
        grid=(n_seq, n_steps),
        in_specs=[pl.BlockSpec(memory_space=pltpu.SMEM), seq_spec((128, GROUP_W))]
                 + [page_spec(i) for i in range(n_pages)] * 2
                 + [seq_spec((128, GROUP_W)), seq_spec((128, GROUP_W)),
                    pl.BlockSpec((128, 128), lambda b, s, pt: (0, 0)),
                    pl.BlockSpec((None, 1, HEAD_V), lambda b, s, pt: (l, 0, 0))],
        out_specs=seq_spec((n_tok, GROUP_W)),
        scratch_shapes=[pltpu.VMEM((128, 1), F32), pltpu.VMEM((128, 1), F32), pltpu.VMEM((128, GROUP_W), F32)],
    )
    return pl.pallas_call(
        functools.partial(_attn_paged_kernel, n_pages=n_pages, past=pages_per_seq * page, out_scale=out_scale),
        grid_spec=grid_spec,
        out_shape=jax.ShapeDtypeStruct((n_seq, n_tok, GROUP_W), F32),
        compiler_params=_cparams(2),
        name="attn_paged",
    )(page_table, scalars, wt, *([cache_k] * n_pages), *([cache_v] * n_pages), k_new, v_new, cinfo, norm_g)


def _paged_query_matrix(qd, n_seq, n_tok):
    q5 = qd.reshape(n_seq, n_tok, N_HEADS, 2, HEAD_QK).transpose(0, 2, 3, 1, 4)
    eye_h = jnp.eye(N_HEADS, dtype=qd.dtype)
    eye_j = jnp.eye(2, dtype=qd.dtype)
    wt = jnp.einsum('bhjtd,hg,ji->bhjtgid', q5, eye_h, eye_j)
    wt = wt.reshape(n_seq, N_HEADS * 2 * n_tok, GROUP_W)
    return jnp.pad(wt, ((0, 0), (0, 128 - wt.shape[1]), (0, 0)))


def _paged_row_info(n_tok):
    info = np.zeros((128, 128), np.float32)
    for h in range(N_HEADS):
        for j in range(2):
            for tkn in range(n_tok):
                c = h * 2 * n_tok + j * n_tok + tkn
                info[c, 0] = 2.0 ** (-8.0 * (h + 1) / N_HEADS)
                info[c, 1] = tkn
    return jnp.asarray(info)


def _outproj_kernel(x_ref, hm_ref, hd_ref, mod_ref, g_ref, w_ref, rwh_ref, rwl_ref, rb_ref,
                    xo_ref, h2_ref, ti_ref, tg_ref):
    mix = jnp.concatenate([hm_ref[...], hd_ref[...]], axis=1)
    xn = x_ref[...] + mod_ref[2] * _mm(mix, w_ref[...])
    xo_ref[...] = xn
    h2 = _rms(xn, g_ref[...]) * (1.0 + mod_ref[4]) + mod_ref[3]
    _to_row_tiles(h2_ref, h2)
    hh = h2.astype(BF16)
    hl = (h2 - hh.astype(F32)).astype(BF16)
    lg = _nt(rwh_ref[...], hh) + _nt(rwh_ref[...], hl) + _nt(rwl_ref[...], hh) + rb_ref[...]
    n_exp = lg.shape[0]
    eidx = lax.broadcasted_iota(I32, lg.shape, 0).astype(F32)
    vals, ids = [], []
    for _ in range(TOP_K):
        mx = jnp.max(lg, axis=0, keepdims=True)
        ik = jnp.min(jnp.where(lg == mx, eidx, float(n_exp)), axis=0, keepdims=True)
        vals.append(mx)
        ids.append(ik)
        lg = jnp.where(eidx == ik, -jnp.inf, lg)
    e = [jnp.exp(v - vals[0]) for v in vals]
    tot = e[0] + e[1] + e[2] + e[3]
    ti_ref[...] = jnp.concatenate(ids, axis=0).astype(I32)
    tg_ref[...] = jnp.concatenate(e, axis=0) / tot


def _outproj(l, x, hm, hd, mod, per_row, rows_per_batch, norm_g, w_out, rw_hi, rw_lo, rb):
    t, d = x.shape
    tm = ROW_TILE
    rspec = lambda w: pl.BlockSpec((tm, w), lambda i: (i, 0))
    wspec = lambda a: pl.BlockSpec((None,) + a.shape[1:], lambda i: (l,) + (0,) * (a.ndim - 1))
    tspec = pl.BlockSpec((TOP_K, tm), lambda i: (0, i))
    return pl.pallas_call(
        _outproj_kernel,
        grid=(t // tm,),
        in_specs=[rspec(d), rspec(GROUP_W), rspec(GROUP_W), _mod_spec(per_row, tm, d, rows_per_batch),
                  wspec(norm_g), wspec(w_out), wspec(rw_hi), wspec(rw_lo), wspec(rb)],
        out_specs=[rspec(d), pl.BlockSpec((tm * ROW_CHUNKS, 128), lambda i: (i, 0)), tspec, tspec],
        out_shape=[jax.ShapeDtypeStruct((t, d), F32), jax.ShapeDtypeStruct((t * ROW_CHUNKS, 128), F32),
                   jax.ShapeDtypeStruct((TOP_K, t), I32), jax.ShapeDtypeStruct((TOP_K, t), F32)],
        compiler_params=_cparams(1),
        name="outproj_norm_router",
    )(x, hm, hd, mod, norm_g, w_out, rw_hi, rw_lo, rb)


def _route(top_i, tm, n_exp):
    t = top_i.shape[1]
    n_assign = t * TOP_K
    flat_e = top_i.T.reshape(-1)
    onehot = (flat_e[:, None] == jnp.arange(n_exp, dtype=I32)[None, :]).astype(I32)
    csum = jnp.cumsum(onehot, axis=0)
    counts = csum[-1]
    rank = jnp.sum(csum * onehot, axis=1) - 1
    padded = (counts + tm - 1) // tm * tm
    pad_end = jnp.cumsum(padded)
    pad_start = pad_end - padded
    pos = (jnp.sum(pad_start[None, :] * onehot, axis=1) + rank).astype(I32)
    n_blocks = -(-(n_assign + n_exp * (tm - 1)) // tm)
    slot_t = jnp.zeros((n_blocks * tm,), I32).at[pos].set(
        jnp.arange(n_assign, dtype=I32) // TOP_K, unique_indices=True)
    block_start = jnp.arange(n_blocks, dtype=I32) * tm
    block_e = jnp.minimum(jnp.sum((pad_end[None, :] <= block_start[:, None]).astype(I32), axis=1),
                          n_exp - 1).astype(I32)
    n_used = (pad_end[-1:] // tm).astype(I32)
    return pos, slot_t, block_e, n_used, n_blocks


def _experts_kernel(slot_ref, be_ref, nu_ref, x_hbm, wgu_ref, wd_ref, bgu_ref, bd_ref,
                    y_ref, xbuf, xbf, wgu_bf, sem, *, tm):
    i = pl.program_id(0)
    n_used = nu_ref[0]
    slot = i % 2

    new_expert = (i == 0) | (be_ref[i] != be_ref[jnp.maximum(i - 1, 0)])

    @pl.when(new_expert & (i < n_used))
    def _():
        wgu_bf[...] = wgu_ref[...].astype(BF16)

    def gather(blk, dst_slot):
        base = blk * tm
        for r in range(tm):
            src = pl.multiple_of(slot_ref[base + r], ROW_CHUNKS)
            pltpu.make_async_copy(x_hbm.at[pl.ds(src, ROW_CHUNKS), :],
                                  xbuf.at[dst_slot, pl.ds(r * ROW_CHUNKS, ROW_CHUNKS), :], sem.at[dst_slot]).start()

    def block(prefetch):
        pltpu.make_async_copy(x_hbm.at[pl.ds(0, tm * ROW_CHUNKS), :], xbuf.at[slot], sem.at[slot]).wait()
        xbf[...] = _from_row_tiles(xbuf.at[slot]).astype(BF16)
        if prefetch:
            gather(i + 1, 1 - slot)
        gu = _mm(xbf[...], wgu_bf[...]) + bgu_ref[...]
        half = gu.shape[1] // 2
        lo, hi = gu[:, :half], gu[:, half:]
        even = (lax.broadcasted_iota(I32, (1, half), 1) & 1) == 0
        g = jnp.where(even, lo, pltpu.roll(hi, 1, 1))
        u = jnp.where(even, pltpu.roll(lo, half - 1, 1), hi)
        g = jnp.minimum(g, SWIGLU_LIMIT)
        u = jnp.clip(u, -SWIGLU_LIMIT, SWIGLU_LIMIT)
        act = (u + 1.0) * (g * jax.nn.sigmoid(SWIGLU_ALPHA * g))
        _to_row_tiles(y_ref, _mm(act.astype(BF16), wd_ref[...]) + bd_ref[...])

    @pl.when(i == 0)
    def _():
        gather(0, 0)

    @pl.when(i + 1 < n_used)
    def _():
        block(True)

    @pl.when(i + 1 == n_used)
    def _():
        block(False)

    @pl.when(i >= n_used)
    def _():
        y_ref[...] = jnp.zeros(y_ref.shape, F32)


def _experts(l, slot_src, block_e, n_used, n_blocks, h2, wgu, wd, bgu, bd):
    tm = ROW_TILE
    wspec = lambda a: pl.BlockSpec((None, None) + a.shape[2:], lambda i, st, be, nu: (l, be[i]) + (0,) * (a.ndim - 2))
    grid_spec = pltpu.PrefetchScalarGridSpec(
        num_scalar_prefetch=3,
        grid=(n_blocks,),
        in_specs=[pl.BlockSpec(memory_space=pl.ANY), wspec(wgu), wspec(wd), wspec(bgu), wspec(bd)],
        out_specs=pl.BlockSpec((tm * ROW_CHUNKS, 128), lambda i, st, be, nu: (i, 0)),
        scratch_shapes=[pltpu.VMEM((2, tm * ROW_CHUNKS, 128), F32), pltpu.VMEM((tm, ROW_CHUNKS * 128), BF16),
                        pltpu.VMEM(wgu.shape[2:], BF16), pltpu.SemaphoreType.DMA((2,))],
    )
    return pl.pallas_call(
        functools.partial(_experts_kernel, tm=tm),
        grid_spec=grid_spec,
        out_shape=jax.ShapeDtypeStruct((n_blocks * tm * ROW_CHUNKS, 128), F32),
        compiler_params=_cparams(1),
        name="experts",
    )(slot_src, block_e, n_used, h2, wgu, wd, bgu, bd)


def _combine_kernel(pos_ref, x_ref, mod_ref, gate_ref, y_hbm, fg_ref, o_ref, buf, sem, *, tm, final):
    i = pl.program_id(0)

    def gather(blk, slot):
        def body(r, carry):
            dst = pl.multiple_of(r * ROW_CHUNKS, ROW_CHUNKS)
            for k in range(TOP_K):
                src = pl.multiple_of(pos_ref[(blk * tm + r) * TOP_K + k], ROW_CHUNKS)
                pltpu.make_async_copy(y_hbm.at[pl.ds(src, ROW_CHUNKS), :],
                                      buf.at[slot, k, pl.ds(dst, ROW_CHUNKS), :], sem.at[slot]).start()
            return carry
        lax.fori_loop(0, tm, body, 0, unroll=4)

    @pl.when(i == 0)
    def _():
        gather(0, 0)

    slot = i % 2

    @pl.when(i + 1 < pl.num_programs(0))
    def _():
        gather(i + 1, 1 - slot)

    for k in range(TOP_K):
        pltpu.make_async_copy(y_hbm.at[pl.ds(0, tm * ROW_CHUNKS), :], buf.at[slot, k], sem.at[slot]).wait()
    gate = gate_ref[...]
    y = gate[:, 0:1] * _from_row_tiles(buf.at[slot, 0])
    for k in range(1, TOP_K):
        y = y + gate[:, k:k + 1] * _from_row_tiles(buf.at[slot, k])
    out = x_ref[...] + mod_ref[5] * y
    o_ref[...] = _rms(out, fg_ref[...]) if final else out


def _combine(pos, x, mod, per_row, rows_per_batch, gate, ys, final_g, final):
    t, d = x.shape
    tm = ROW_TILE
    grid_spec = pltpu.PrefetchScalarGridSpec(
        num_scalar_prefetch=1,
        grid=(t // tm,),
        in_specs=[pl.BlockSpec((tm, d), lambda i, p: (i, 0)),
                  _wrap_prefetch(_mod_spec(per_row, tm, d, rows_per_batch)),
                  pl.BlockSpec((tm, TOP_K), lambda i, p: (i, 0)),
                  pl.BlockSpec(memory_space=pl.ANY),
                  pl.BlockSpec((1, d), lambda i, p: (0, 0))],
        out_specs=pl.BlockSpec((tm, d), lambda i, p: (i, 0)),
        scratch_shapes=[pltpu.VMEM((2, TOP_K, tm * ROW_CHUNKS, 128), F32), pltpu.SemaphoreType.DMA((2,))],
    )
    return pl.pallas_call(
        functools.partial(_combine_kernel, tm=tm, final=final),
        grid_spec=grid_spec,
        out_shape=jax.ShapeDtypeStruct((t, d), F32),
        compiler_params=_cparams(1),
        name="combine",
    )(pos, x, mod, gate, ys, final_g)


def _wrap_prefetch(spec):
    return pl.BlockSpec(spec.block_shape, lambda i, p: spec.index_map(i))


def kernel(x_prompt, x_sample, c_prompt, c_sample, cache_k, cache_v, state_C, state_n, state_m, page_table, norm1_g, ada_w, ada_b, w_in, b_igate, b_fgate, mlstm_norm_g, lam_q1, lam_k1, lam_q2, lam_k2, diff_norm_g, w_out, norm2_g, router_w, router_b, w_gu, b_gu, w_down, b_down, final_norm_g):
    n_layers = ada_w.shape[0]
    bp, seq, d = x_prompt.shape
    bd, n_tok, _ = x_sample.shape
    n_exp = router_w.shape[-1]
    tp, ts = bp * seq, bd * n_tok
    nqk = N_HEADS * HEAD_QK
    assert d == 2 * GROUP_W and n_tok == 8 and seq % MLSTM_CHUNK == 0 and ts % ROW_TILE == 0
    assert w_in.shape[-1] == 2 * nqk + GROUP_W + 2 * N_HEADS + 4 * GROUP_W

    g_lo = 2 * nqk + GROUP_W
    wa = w_in[:, :, :g_lo].astype(BF16)
    wb = w_in[:, :, g_lo + 2 * N_HEADS:].astype(BF16)
    w_gate = w_in[:, :, g_lo:g_lo + 2 * N_HEADS]
    wg_col = jnp.pad(w_gate, ((0, 0), (0, 0), (0, 128 - 2 * N_HEADS))).astype(BF16)
    wg_row = jnp.swapaxes(w_gate, 1, 2).astype(BF16)
    b_gate = jnp.concatenate([b_igate, b_fgate], axis=1).astype(F32)
    bg_col = jnp.pad(b_gate, ((0, 0), (0, 128 - 2 * N_HEADS)))[:, None, :]
    bg_row = b_gate[:, :, None]
    w_out_bf = w_out.astype(BF16)
    rw_t = jnp.swapaxes(router_w, 1, 2).astype(F32)
    rw_hi = rw_t.astype(BF16)
    rw_lo = (rw_t - rw_hi.astype(F32)).astype(BF16)
    rb = router_b.astype(F32)[:, :, None]
    n_ff = w_down.shape[2]
    w_d = (w_down.reshape(n_layers, n_exp, 2, n_ff // 2, d).swapaxes(2, 3)
           .reshape(n_layers, n_exp, n_ff, d).astype(BF16))
    b_gu_f = b_gu[..., None, :].astype(F32)
    b_d = b_down[..., None, :].astype(F32)
    n1g, n2g = norm1_g[:, None, :], norm2_g[:, None, :]
    mng, dng = mlstm_norm_g[:, None, :], diff_norm_g[:, None, :]
    fng = final_norm_g[None, :]
    slopes = jnp.asarray(2.0 ** (-8.0 * np.arange(1, N_HEADS + 1) / N_HEADS), F32)
    assert cache_k.shape[3:] == (N_HEADS, HEAD_V) and cache_v.shape[3:] == (N_HEADS, HEAD_V)
    pool, page = cache_k.shape[1], cache_k.shape[2]
    ck = cache_k.reshape(n_layers, pool, page * N_HEADS, HEAD_V)
    cv = cache_v.reshape(n_layers, pool, page * N_HEADS, HEAD_V)
    cinfo = _paged_row_info(n_tok)

    mod_all = _adaln_mod(jnp.concatenate([c_prompt, c_sample], axis=0), ada_w, ada_b)

    xp = x_prompt.reshape(tp, d)
    xs = x_sample.reshape(ts, d)
    zero_state = _pad_state(jnp.zeros((bp, N_HEADS, HEAD_V, HEAD_QK), F32), jnp.zeros((bp, N_HEADS, HEAD_QK), F32),
                            jnp.zeros((bp, N_HEADS), F32))
    n_chunks = seq // MLSTM_CHUNK
    outs = []
    for l in range(n_layers):
        lam_init = 0.8 - 0.6 * math.exp(-0.3 * l)
        lam = (jnp.exp(jnp.sum(lam_q1[l].astype(F32) * lam_k1[l].astype(F32)))
               - jnp.exp(jnp.sum(lam_q2[l].astype(F32) * lam_k2[l].astype(F32))) + lam_init)
        scalars = jnp.concatenate([lam[None], slopes]).astype(F32)
        mod_p = mod_all[l, :bp].reshape(bp, 6, 1, d)
        mod_s = jnp.repeat(mod_all[l, bp:].reshape(bd, 6, d).transpose(1, 0, 2), n_tok, axis=1)
        proj_w = (n1g, wa, wb, wg_col, wg_row, bg_col, bg_row)

        qk, vm, og, qd, kd_p, vd_p, gcol, grow = _inproj(l, xp, mod_p, False, seq, *proj_w)
        grow = grow.reshape(tp // ROW_TILE, 2 * N_HEADS, ROW_TILE // MLSTM_CHUNK, MLSTM_CHUNK)
        grow = grow.transpose(0, 2, 1, 3).reshape(tp // MLSTM_CHUNK, 2 * N_HEADS, MLSTM_CHUNK)
        hm_p, c_p, n_p, m_p = _mlstm(l, qk, vm, og, gcol, grow, *zero_state, mng, bp, n_chunks, MLSTM_CHUNK)
        hd_p = _attn_prompt(l, scalars, qd, kd_p, vd_p, dng, bp, seq, 1.0 - lam_init)

        qk, vm, og, qd, kd_s, vd_s, gcol, grow = _inproj(l, xs, mod_s, True, 0, *proj_w)
        pad_rows = lambda a: jnp.pad(a.reshape(bd, n_tok, a.shape[-1]),
                                     ((0, 0), (0, MLSTM_PAD - n_tok), (0, 0)))
        gcol = gcol.reshape(bd, n_tok, 128)
        gcol = jnp.concatenate(
            [gcol, jnp.broadcast_to(jnp.where(jnp.arange(128) < N_HEADS, NEG_BIG, 0.0).astype(F32),
                                    (bd, MLSTM_PAD - n_tok, 128))], axis=1).reshape(bd * MLSTM_PAD, 128)
        grow = grow.transpose(1, 0, 2).reshape(2 * N_HEADS, bd, n_tok).transpose(1, 0, 2)
        grow = jnp.concatenate(
            [grow, jnp.broadcast_to(jnp.where(jnp.arange(2 * N_HEADS) < N_HEADS, NEG_BIG, 0.0).astype(F32)[None, :, None],
                                    (bd, 2 * N_HEADS, MLSTM_PAD - n_tok))], axis=2)
        flat = lambda a: pad_rows(a).reshape(bd * MLSTM_PAD, a.shape[-1])
        hm_s, c_s, n_s, m_s = _mlstm(l, flat(qk), flat(vm), flat(og), gcol, grow,
                                     *_pad_state(state_C[l], state_n[l], state_m[l]), mng, bd, 1, MLSTM_PAD)
        hm_s = hm_s.reshape(bd, MLSTM_PAD, GROUP_W)[:, :n_tok].reshape(ts, GROUP_W)
        wt = _paged_query_matrix(qd, bd, n_tok)
        hd_s = _attn_paged(l, page_table, scalars, wt, ck, cv, pad_rows(kd_s), pad_rows(vd_s), cinfo, dng,
                           1.0 - lam_init)
        hd_s = hd_s.reshape(ts, GROUP_W).astype(BF16)

        post_w = (n2g, w_out_bf, rw_hi, rw_lo, rb)
        xp, h2_p, ti_p, tg_p = _outproj(l, xp, hm_p, hd_p, mod_p, False, seq, *post_w)
        xs, h2_s, ti_s, tg_s = _outproj(l, xs, hm_s, hd_s, mod_s, True, 0, *post_w)
        h2 = jnp.concatenate([h2_p, h2_s], axis=0)
        pos, slot_t, block_e, n_used, n_blocks = _route(jnp.concatenate([ti_p, ti_s], axis=1), ROW_TILE, n_exp)
        ys = _experts(l, slot_t * ROW_CHUNKS, block_e, n_used, n_blocks, h2, w_gu, w_d, b_gu_f, b_d)
        final = l == n_layers - 1
        pos = pos * ROW_CHUNKS
        xp = _combine(pos[:tp * TOP_K], xp, mod_p, False, seq, tg_p.T, ys, fng, final)
        xs = _combine(pos[tp * TOP_K:], xs, mod_s, True, 0, tg_s.T, ys, fng, final)

        outs.append((kd_p, vd_p, c_p, n_p, m_p, kd_s, vd_s, c_s, n_s, m_s))

    def stacked(i, shape, dtype):
        return jnp.stack([o[i] for o in outs]).reshape((n_layers,) + shape).astype(dtype)

    states_p = [_unpad_state(o[2], o[3], o[4]) for o in outs]
    states_s = [_unpad_state(o[7], o[8], o[9]) for o in outs]
    sdt = state_C.dtype
    st = lambda states, i: jnp.stack([s[i] for s in states]).astype(sdt)
    return (xp.reshape(bp, seq, d), xs.reshape(bd, n_tok, d),
            stacked(0, (bp, seq, N_HEADS, 2 * HEAD_QK), cache_k.dtype),
            stacked(1, (bp, seq, N_HEADS, HEAD_V), cache_v.dtype),
            st(states_p, 0), st(states_p, 1), st(states_p, 2),
            stacked(5, (bd, n_tok, N_HEADS, 2 * HEAD_QK), cache_k.dtype),
            stacked(6, (bd, n_tok, N_HEADS, HEAD_V), cache_v.dtype),
            st(states_s, 0), st(states_s, 1), st(states_s, 2))
```

```python
import functools
import math

import numpy as np
import jax
import jax.numpy as jnp
from jax import lax
from jax.experimental import pallas as pl
from jax.experimental.pallas import tpu as pltpu

F32 = jnp.float32
BF16 = jnp.bfloat16
I32 = jnp.int32

NORM_EPS = 1e-6
N_HEADS = 4
HEAD_V = 128
HEAD_QK = 64
GROUP_W = N_HEADS * HEAD_V
TOP_K = 4
SWIGLU_LIMIT = 7.0
SWIGLU_ALPHA = 1.702
NEG_BIG = -1e30

ROW_TILE = 256
MLSTM_CHUNK = 256
MLSTM_PAD = 128
ATTN_TILE = 256
ATTN_CHUNK = 128
PAGES_PER_STEP = 8
GATHER_SLOTS = 3
VMEM_LIMIT = 48 * 1024 * 1024
EXPERTS_VMEM_LIMIT = 56 * 1024 * 1024


def _cparams(n_axes, vmem=VMEM_LIMIT):
    return pltpu.CompilerParams(dimension_semantics=("arbitrary",) * n_axes, vmem_limit_bytes=vmem)


def _nt(a, b):
    return lax.dot_general(a, b, (((1,), (1,)), ((), ())), preferred_element_type=F32)


def _mm(a, b):
    return jnp.dot(a, b, preferred_element_type=F32)


def _split3(x):
    hi = x.astype(BF16)
    r1 = x - hi.astype(F32)
    mid = r1.astype(BF16)
    lo = (r1 - mid.astype(F32)).astype(BF16)
    return hi, mid, lo


def _log_sigmoid(x):
    return jnp.minimum(x, 0.0) - jnp.log1p(jnp.exp(-jnp.abs(x)))


def _rms(x, g):
    return x * lax.rsqrt(jnp.mean(x * x, axis=-1, keepdims=True) + NORM_EPS) * g


ROW_CHUNKS = 8


def _to_row_tiles(ref, x):
    rows, c_n = x.shape[0], x.shape[1] // 128
    for c in range(c_n):
        ref[pl.ds(c, rows, stride=c_n), :] = x[:, c * 128:(c + 1) * 128]


def _from_row_tiles(ref, c_n=ROW_CHUNKS):
    rows = ref.shape[0] // c_n
    return jnp.concatenate([ref[pl.ds(c, rows, stride=c_n), :] for c in range(c_n)], axis=1)


def _mod_kernel(c_ref, w_ref, b_ref, o_ref):
    c = c_ref[...]
    s = c * jax.nn.sigmoid(c)
    o_ref[...] = _mm(s.astype(BF16), w_ref[...].astype(BF16)) + b_ref[...]


def _adaln_mod(c_all, ada_w, ada_b):
    n_layers, d, n = ada_w.shape
    bc = c_all.shape[0]
    tn = n // 4
    return pl.pallas_call(
        _mod_kernel,
        grid=(n_layers, n // tn),
        in_specs=[pl.BlockSpec((bc, d), lambda l, j: (0, 0)),
                  pl.BlockSpec((None, d, tn), lambda l, j: (l, 0, j)),
                  pl.BlockSpec((None, 1, tn), lambda l, j: (l, 0, j))],
        out_specs=pl.BlockSpec((None, bc, tn), lambda l, j: (l, 0, j)),
        out_shape=jax.ShapeDtypeStruct((n_layers, bc, n), F32),
        compiler_params=_cparams(2),
        name="adaln_mod",
    )(c_all, ada_w, ada_b.reshape(n_layers, 1, n))


def _mod_spec(per_row, tm, d, rows_per_batch):
    if per_row:
        return pl.BlockSpec((6, tm, d), lambda i: (0, i, 0))
    tiles_per_batch = rows_per_batch // tm
    return pl.BlockSpec((None, 6, 1, d), lambda i: (i // tiles_per_batch, 0, 0, 0))


def _inproj_kernel(x_ref, mod_ref, g_ref, wa_ref, wb_ref, wg_ref, wgt_ref, bgc_ref, bgr_ref,
                   qk_ref, vm_ref, og_ref, qd_ref, kd_ref, vd_ref, gcol_ref, grow_ref):
    h = (_rms(x_ref[...], g_ref[...]) * (1.0 + mod_ref[1]) + mod_ref[0]).astype(BF16)
    a = _mm(h, wa_ref[...])
    lane = lax.broadcasted_iota(I32, (1, GROUP_W), 1)
    k_scale = jnp.where(lane >= N_HEADS * HEAD_QK, HEAD_QK ** -0.5, 1.0)
    qk_ref[...] = (a[:, :GROUP_W] * k_scale).astype(BF16)
    vm_ref[...] = a[:, GROUP_W:].astype(BF16)
    b = _mm(h, wb_ref[...])
    og_ref[...] = jax.nn.sigmoid(b[:, :GROUP_W]).astype(BF16)
    qd_ref[...] = (b[:, GROUP_W:2 * GROUP_W] * HEAD_QK ** -0.5).astype(BF16)
    kd_ref[...] = b[:, 2 * GROUP_W:3 * GROUP_W]
    vd_ref[...] = b[:, 3 * GROUP_W:]
    gc = _mm(h, wg_ref[...]) + bgc_ref[...]
    lane_g = lax.broadcasted_iota(I32, gc.shape, 1)
    gcol_ref[...] = jnp.where((lane_g >= N_HEADS) & (lane_g < 2 * N_HEADS), _log_sigmoid(gc), gc)
    gr = _nt(wgt_ref[...], h) + bgr_ref[...]
    row_g = lax.broadcasted_iota(I32, gr.shape, 0)
    grow_ref[...] = jnp.where(row_g >= N_HEADS, _log_sigmoid(gr), gr)


def _inproj(l, x, mod, per_row, rows_per_batch, norm_g, wa, wb, wg, wgt, bgc, bgr):
    t, d = x.shape
    tm = ROW_TILE
    nt = t // tm
    row = lambda w, dt: jax.ShapeDtypeStruct((t, w), dt)
    rspec = lambda w: pl.BlockSpec((tm, w), lambda i: (i, 0))
    wspec = lambda a: pl.BlockSpec((None,) + a.shape[1:], lambda i: (l,) + (0,) * (a.ndim - 1))
    return pl.pallas_call(
        _inproj_kernel,
        grid=(nt,),
        in_specs=[rspec(d), _mod_spec(per_row, tm, d, rows_per_batch), wspec(norm_g),
                  wspec(wa), wspec(wb), wspec(wg), wspec(wgt), wspec(bgc), wspec(bgr)],
        out_specs=[rspec(GROUP_W)] * 6 + [rspec(128), pl.BlockSpec((None, 2 * N_HEADS, tm), lambda i: (i, 0, 0))],
        out_shape=[row(GROUP_W, BF16)] * 4 + [row(GROUP_W, F32)] * 2
                  + [row(128, F32), jax.ShapeDtypeStruct((nt, 2 * N_HEADS, tm), F32)],
        compiler_params=_cparams(1),
        name="norm_inproj",
    )(x, mod, norm_g, wa, wb, wg, wgt, bgc, bgr)


def _mlstm_kernel(qk_ref, v_ref, og_ref, gc_ref, gr_ref, c0_ref, n0_ref, m0_ref, g_ref,
                  h_ref, c_ref, n_ref, m_ref, *, chunk):
    @pl.when(pl.program_id(1) == 0)
    def _():
        c_ref[...] = c0_ref[...]
        n_ref[...] = n0_ref[...]
        m_ref[...] = m0_ref[...]

    gc = gc_ref[...]
    gr = gr_ref[...]
    r = lax.broadcasted_iota(I32, (chunk, chunk), 0)
    s = lax.broadcasted_iota(I32, (chunk, chunk), 1)
    causal = r >= s
    tri = causal.astype(BF16)
    tri_t = (r <= s).astype(BF16)
    lane_g = lax.broadcasted_iota(I32, gc.shape, 1)
    gl = jnp.where(lane_g >= N_HEADS, gc, 0.0)
    cum_c = sum(_mm(tri, p) for p in _split3(gl))
    cum_r = sum(_mm(p, tri_t) for p in _split3(gr))
    lane = lax.broadcasted_iota(I32, (1, HEAD_V), 1)

    for h in range(N_HEADS):
        t0 = (h // 2) * HEAD_V
        off = (h % 2) * HEAD_QK
        hmask = (lane >= off) & (lane < off + HEAD_QK)
        zero = jnp.zeros((), BF16)
        qh = jnp.where(hmask, qk_ref[:, t0:t0 + HEAD_V], zero)
        kh = jnp.where(hmask, qk_ref[:, N_HEADS * HEAD_QK + t0:N_HEADS * HEAD_QK + t0 + HEAD_V], zero)
        hs = slice(h * HEAD_V, (h + 1) * HEAD_V)
        vh = v_ref[:, hs]
        ig_r = gr[h:h + 1, :]
        b_r = cum_r[N_HEADS + h:N_HEADS + h + 1, :]
        ig_c = gc[:, h:h + 1]
        b_c = cum_c[:, N_HEADS + h:N_HEADS + h + 1]
        m_prev = m_ref[h:h + 1, 0:1]
        c_prev = c_ref[h]
        n_prev = n_ref[h:h + 1, :]

        dmat = jnp.where(causal, b_c - b_r + ig_r, -jnp.inf)
        inter = b_c + m_prev
        m_row = jnp.maximum(jnp.max(dmat, axis=1, keepdims=True), inter)
        w_intra = _nt(qh, kh) * jnp.exp(dmat - m_row)
        w_inter = jnp.exp(inter - m_row)
        num = _mm(w_intra.astype(BF16), vh) + w_inter * _nt(qh, c_prev.astype(BF16))
        den = (jnp.sum(w_intra, axis=1, keepdims=True)
               + w_inter * jnp.sum(qh.astype(F32) * n_prev, axis=1, keepdims=True))
        hv = num / jnp.maximum(jnp.abs(den), jnp.exp(-m_row))
        h_ref[:, hs] = (_rms(hv, g_ref[:, hs]) * og_ref[:, hs].astype(F32)).astype(BF16)

        b_end = b_c[chunk - 1:chunk, :]
        w_end = b_end - b_r + ig_r
        m_new = jnp.maximum(b_end + m_prev, jnp.max(w_end, axis=1, keepdims=True))
        decay = jnp.exp(b_end + m_prev - m_new)
        kw = kh.astype(F32) * jnp.exp(b_end - b_c + ig_c - m_new)
        v_t = vh.astype(F32).T.astype(BF16)
        c_ref[h] = decay * c_prev + _mm(v_t, kw.astype(BF16))
        n_ref[h:h + 1, :] = decay * n_prev + jnp.sum(kw, axis=0, keepdims=True)
        m_ref[h:h + 1, :] = jnp.broadcast_to(m_new, (1, HEAD_V))


def _mlstm(l, qk, vm, og, gcol, grow, c0, n0, m0, norm_g, n_seq, n_chunks, chunk):
    t = qk.shape[0]
    rspec = lambda w: pl.BlockSpec((chunk, w), lambda b, c: (b * n_chunks + c, 0))
    sspec = lambda shp: pl.BlockSpec((None,) + shp, lambda b, c: (b,) + (0,) * len(shp))
    c_shape, s_shape = (N_HEADS, HEAD_V, HEAD_V), (2 * N_HEADS, HEAD_V)
    return pl.pallas_call(
        functools.partial(_mlstm_kernel, chunk=chunk),
        grid=(n_seq, n_chunks),
        in_specs=[rspec(GROUP_W), rspec(GROUP_W), rspec(GROUP_W), rspec(128),
                  pl.BlockSpec((None, 2 * N_HEADS, chunk), lambda b, c: (b * n_chunks + c, 0, 0)),
                  sspec(c_shape), sspec(s_shape), sspec(s_shape),
                  pl.BlockSpec((None, 1, GROUP_W), lambda b, c: (l, 0, 0))],
        out_specs=[rspec(GROUP_W), sspec(c_shape), sspec(s_shape), sspec(s_shape)],
        out_shape=[jax.ShapeDtypeStruct((t, GROUP_W), BF16),
                   jax.ShapeDtypeStruct((n_seq,) + c_shape, F32),
                   jax.ShapeDtypeStruct((n_seq,) + s_shape, F32),
                   jax.ShapeDtypeStruct((n_seq,) + s_shape, F32)],
        compiler_params=_cparams(2),
        name="mlstm",
    )(qk, vm, og, gcol, grow, c0, n0, m0, norm_g)


def _pad_state(c, n, m):
    b = c.shape[0]
    c_pad = jnp.zeros((b, N_HEADS, HEAD_V, HEAD_V), F32)
    n_pad = jnp.zeros((b, 2 * N_HEADS, HEAD_V), F32)
    for h in range(N_HEADS):
        off = (h % 2) * HEAD_QK
        c_pad = c_pad.at[:, h, :, off:off + HEAD_QK].set(c[:, h].astype(F32))
        n_pad = n_pad.at[:, h, off:off + HEAD_QK].set(n[:, h].astype(F32))
    m_pad = jnp.zeros((b, 2 * N_HEADS, HEAD_V), F32).at[:, :N_HEADS, :].set(
        jnp.broadcast_to(m.astype(F32)[:, :, None], (b, N_HEADS, HEAD_V)))
    return c_pad, n_pad, m_pad


def _unpad_state(c_pad, n_pad, m_pad):
    c = jnp.stack([c_pad[:, h, :, (h % 2) * HEAD_QK:(h % 2) * HEAD_QK + HEAD_QK] for h in range(N_HEADS)], 1)
    n = jnp.stack([n_pad[:, h, (h % 2) * HEAD_QK:(h % 2) * HEAD_QK + HEAD_QK] for h in range(N_HEADS)], 1)
    return c, n, m_pad[:, :N_HEADS, 0]


def _head_finish(o1, o2, lam, g, out_scale):
    d = o1 - lam * o2
    return _rms(d, g) * out_scale


def _attn_prompt_kernel(sc_ref, q_ref, k_ref, v_ref, g_ref, o_ref, kb_ref, vb_ref, q2_ref, *state,
                        tile, chunk, out_scale):
    h = pl.program_id(1)
    qi = pl.program_id(2)
    n_chunks = 2 * tile // chunk
    m_refs, l_refs, acc_refs = state[0::3], state[1::3], state[2::3]

    @pl.when(qi == 0)
    def _():
        kb_ref[...] = k_ref[...].astype(BF16)
        vb_ref[...] = v_ref[...].astype(BF16)

    lam = sc_ref[0]
    slope = sc_ref[1 + h]
    q = q_ref[...]
    lane = lax.broadcasted_iota(I32, (1, HEAD_V), 1)
    zero = jnp.zeros((), BF16)
    q2_ref[0:tile, :] = jnp.where(lane < HEAD_QK, q, zero)
    q2_ref[tile:2 * tile, :] = jnp.where(lane >= HEAD_QK, q, zero)
    for c in range(n_chunks):
        m_refs[c][...] = jnp.full((chunk, HEAD_V), -jnp.inf, F32)
        l_refs[c][...] = jnp.zeros((chunk, HEAD_V), F32)
        acc_refs[c][...] = jnp.zeros((chunk, HEAD_V), F32)
    col = lax.broadcasted_iota(I32, (1, tile), 1)
    ones = jnp.ones((tile, HEAD_V), BF16)

    def qk(j):
        ks = kb_ref[pl.ds(pl.multiple_of(j * tile, tile), tile), :]
        return tuple(_nt(q2_ref[c * chunk:(c + 1) * chunk, :], ks) for c in range(n_chunks))

    def step(j, scores, diagonal):
        v_ones = jnp.concatenate([vb_ref[pl.ds(pl.multiple_of(j * tile, tile), tile), :], ones], axis=1)
        bias = slope * (col + (j - qi) * tile).astype(F32)
        for c in range(n_chunks):
            sc = scores[c] + bias
            if diagonal:
                row = lax.broadcasted_iota(I32, (chunk, 1), 0) + (c * chunk) % tile
                sc = jnp.where(col <= row, sc, -jnp.inf)
            m_old = m_refs[c][...]
            m_new = jnp.maximum(m_old, jnp.max(sc, axis=1, keepdims=True))
            alpha = jnp.exp(m_old - m_new)
            p = jnp.exp(sc - jnp.concatenate([m_new] * (tile // HEAD_V), axis=1)).astype(BF16)
            pv = _mm(p, v_ones)
            acc_refs[c][...] = alpha * acc_refs[c][...] + pv[:, :HEAD_V]
            l_refs[c][...] = alpha * l_refs[c][...] + pv[:, HEAD_V:]
            m_refs[c][...] = m_new

    def body(j, scores):
        nxt = qk(j + 1)
        step(j, scores, False)
        return nxt

    scores = lax.fori_loop(0, qi, body, qk(0))
    step(qi, scores, True)
    o = jnp.concatenate([acc_refs[c][...] / l_refs[c][...] for c in range(n_chunks)], axis=0)
    o_ref[...] = _head_finish(o[:tile], o[tile:], lam, g_ref[...], out_scale).astype(BF16)


def _attn_prompt(l, scalars, qd, kd, vd, norm_g, n_seq, seq, out_scale):
    t = qd.shape[0]
    tile = min(ATTN_TILE, seq)
    chunk = min(ATTN_CHUNK, tile)
    nq = seq // tile
    return pl.pallas_call(
        functools.partial(_attn_prompt_kernel, tile=tile, chunk=chunk, out_scale=out_scale),
        grid=(n_seq, N_HEADS, nq),
        in_specs=[pl.BlockSpec(memory_space=pltpu.SMEM),
                  pl.BlockSpec((tile, HEAD_V), lambda b, h, i: (b * nq + i, h)),
                  pl.BlockSpec((seq, HEAD_V), lambda b, h, i: (b, h)),
                  pl.BlockSpec((seq, HEAD_V), lambda b, h, i: (b, h)),
                  pl.BlockSpec((None, 1, HEAD_V), lambda b, h, i: (l, 0, 0))],
        out_specs=pl.BlockSpec((tile, HEAD_V), lambda b, h, i: (b * nq + i, h)),
        out_shape=jax.ShapeDtypeStruct((t, GROUP_W), BF16),
        scratch_shapes=[pltpu.VMEM((seq, HEAD_V), BF16), pltpu.VMEM((seq, HEAD_V), BF16),
                        pltpu.VMEM((2 * tile, HEAD_V), BF16)]
                       + [pltpu.VMEM((chunk, HEAD_V), F32)] * (3 * (2 * tile // chunk)),
        compiler_params=_cparams(3),
        name="attn_prompt",
    )(scalars, qd, kd, vd, norm_g)


def _attn_paged_kernel(pt_ref, sc_ref, wt_ref, *refs, n_pages, past, out_scale):
    del pt_ref
    k_refs, v_refs = refs[:n_pages], refs[n_pages:2 * n_pages]
    kn_ref, vn_ref, ci_ref, g_ref, o_ref, m_ref, l_ref, acc_ref = refs[2 * n_pages:]
    s = pl.program_id(1)
    n_keys = n_pages * k_refs[0].shape[0] // N_HEADS

    @pl.when(s == 0)
    def _():
        m_ref[...] = jnp.full(m_ref.shape, -jnp.inf, F32)
        l_ref[...] = jnp.zeros(l_ref.shape, F32)
        acc_ref[...] = jnp.zeros(acc_ref.shape, F32)

    wt = wt_ref[...]
    slope = ci_ref[:, 0:1]
    tok = ci_ref[:, 1:2]

    def update(sc, v_bf):
        m_old = m_ref[...]
        m_new = jnp.maximum(m_old, jnp.max(sc, axis=1, keepdims=True))
        alpha = jnp.exp(m_old - m_new)
        p = jnp.exp(sc - m_new)
        l_ref[...] = alpha * l_ref[...] + jnp.sum(p, axis=1, keepdims=True)
        acc_ref[...] = alpha * acc_ref[...] + _mm(p.astype(BF16), v_bf)
        m_ref[...] = m_new

    def pages(page_refs):
        return jnp.concatenate([_from_row_tiles(r, N_HEADS) for r in page_refs], axis=0).astype(BF16)

    k_bf = pages(k_refs)
    v_bf = pages(v_refs)
    kpos = lax.broadcasted_iota(I32, (1, n_keys), 1) + (s * n_keys - past)
    update(_nt(wt, k_bf) + slope * (kpos.astype(F32) - tok), v_bf)

    @pl.when(s == pl.num_programs(1) - 1)
    def _():
        r = lax.broadcasted_iota(I32, (1, kn_ref.shape[0]), 1).astype(F32)
        sc = _nt(wt, kn_ref[...].astype(BF16)) + slope * (r - tok)
        update(jnp.where(r <= tok, sc, -jnp.inf), vn_ref[...].astype(BF16))
        o = acc_ref[...] / l_ref[...]
        lam = sc_ref[0]
        n_tok = o_ref.shape[0]
        for h in range(N_HEADS):
            hs = slice(h * HEAD_V, (h + 1) * HEAD_V)
            c0 = h * 2 * n_tok
            o_ref[:, hs] = _head_finish(o[c0:c0 + n_tok, hs], o[c0 + n_tok:c0 + 2 * n_tok, hs],
                                        lam, g_ref[...], out_scale)


def _attn_paged(l, page_table, scalars, wt, cache_k, cache_v, k_new, v_new, cinfo, norm_g, out_scale):
    n_seq, pages_per_seq = page_table.shape
    page = cache_k.shape[2] // N_HEADS
    n_pages = min(PAGES_PER_STEP, pages_per_seq)
    n_steps = pages_per_seq // n_pages
    n_tok = 8

    def page_spec(i):
        return pl.BlockSpec((None, None, page * N_HEADS, HEAD_V),
                            lambda b, s, pt: (l, pt[b, s * n_pages + i], 0, 0))

    seq_spec = lambda shp: pl.BlockSpec((None,) + shp, lambda b, s, pt: (b,) + (0,) * len(shp))
    grid_spec = pltpu.PrefetchScalarGridSpec(
        num_scalar_prefetch=1,
        grid=(n_seq, n_steps),
        in_specs=[pl.BlockSpec(memory_space=pltpu.SMEM), seq_spec((128, GROUP_W))]
                 + [page_spec(i) for i in range(n_pages)] * 2
                 + [seq_spec((128, GROUP_W)), seq_spec((128, GROUP_W)),
                    pl.BlockSpec((128, 128), lambda b, s, pt: (0, 0)),
                    pl.BlockSpec((None, 1, HEAD_V), lambda b, s, pt: (l, 0, 0))],
        out_specs=seq_spec((n_tok, GROUP_W)),
        scratch_shapes=[pltpu.VMEM((128, 1), F32), pltpu.VMEM((128, 1), F32), pltpu.VMEM((128, GROUP_W), F32)],
    )
    return pl.pallas_call(
        functools.partial(_attn_paged_kernel, n_pages=n_pages, past=pages_per_seq * page, out_scale=out_scale),
        grid_spec=grid_spec,
        out_shape=jax.ShapeDtypeStruct((n_seq, n_tok, GROUP_W), F32),
        compiler_params=_cparams(2),
        name="attn_paged",
    )(page_table, scalars, wt, *([cache_k] * n_pages), *([cache_v] * n_pages), k_new, v_new, cinfo, norm_g)


def _paged_query_matrix(qd, n_seq, n_tok):
    q5 = qd.reshape(n_seq, n_tok, N_HEADS, 2, HEAD_QK).transpose(0, 2, 3, 1, 4)
    eye_h = jnp.eye(N_HEADS, dtype=qd.dtype)
    eye_j = jnp.eye(2, dtype=qd.dtype)
    wt = jnp.einsum('bhjtd,hg,ji->bhjtgid', q5, eye_h, eye_j)
    wt = wt.reshape(n_seq, N_HEADS * 2 * n_tok, GROUP_W)
    return jnp.pad(wt, ((0, 0), (0, 128 - wt.shape[1]), (0, 0)))


def _paged_row_info(n_tok):
    info = np.zeros((128, 128), np.float32)
    for h in range(N_HEADS):
        for j in range(2):
            for tkn in range(n_tok):
                c = h * 2 * n_tok + j * n_tok + tkn
                info[c, 0] = 2.0 ** (-8.0 * (h + 1) / N_HEADS)
                info[c, 1] = tkn
    return jnp.asarray(info)


def _outproj_kernel(x_ref, hm_ref, hd_ref, mod_ref, g_ref, w_ref, rwh_ref, rwl_ref, rb_ref, cnt0_ref,
                    xo_ref, h2_ref, ti_ref, tg_ref, rk_ref, cnt_ref):
    @pl.when(pl.program_id(0) == 0)
    def _():
        cnt_ref[...] = cnt0_ref[...]

    mix = jnp.concatenate([hm_ref[...], hd_ref[...]], axis=1)
    xn = x_ref[...] + mod_ref[2] * _mm(mix, w_ref[...])
    xo_ref[...] = xn
    h2 = _rms(xn, g_ref[...]) * (1.0 + mod_ref[4]) + mod_ref[3]
    _to_row_tiles(h2_ref, h2)
    hh = h2.astype(BF16)
    hl = (h2 - hh.astype(F32)).astype(BF16)
    lg = _nt(rwh_ref[...], hh) + _nt(rwh_ref[...], hl) + _nt(rwl_ref[...], hh) + rb_ref[...]
    n_exp = lg.shape[0]
    eidx = lax.broadcasted_iota(I32, lg.shape, 0).astype(F32)
    vals, ids = [], []
    for _ in range(TOP_K):
        mx = jnp.max(lg, axis=0, keepdims=True)
        ik = jnp.min(jnp.where(lg == mx, eidx, float(n_exp)), axis=0, keepdims=True)
        vals.append(mx)
        ids.append(ik)
        lg = jnp.where(eidx == ik, -jnp.inf, lg)
    e = [jnp.exp(v - vals[0]) for v in vals]
    tot = e[0] + e[1] + e[2] + e[3]
    ti_ref[...] = jnp.concatenate(ids, axis=0).astype(I32)
    tg_ref[...] = jnp.concatenate(e, axis=0) / tot
    tm = lg.shape[1]
    member = sum((eidx == ik).astype(F32) for ik in ids)
    earlier = (lax.broadcasted_iota(I32, (tm, tm), 0) < lax.broadcasted_iota(I32, (tm, tm), 1)).astype(BF16)
    seen = cnt_ref[:, 0:1] + _mm(member.astype(BF16), earlier)
    rk_ref[...] = jnp.concatenate(
        [jnp.sum(jnp.where(eidx == ik, seen, 0.0), axis=0, keepdims=True) for ik in ids], axis=0).astype(I32)
    cnt_ref[...] = cnt_ref[...] + jnp.sum(member, axis=1, keepdims=True)


def _outproj(l, x, hm, hd, mod, per_row, rows_per_batch, norm_g, w_out, rw_hi, rw_lo, rb, counts):
    t, d = x.shape
    tm = ROW_TILE
    rspec = lambda w: pl.BlockSpec((tm, w), lambda i: (i, 0))
    wspec = lambda a: pl.BlockSpec((None,) + a.shape[1:], lambda i: (l,) + (0,) * (a.ndim - 1))
    tspec = pl.BlockSpec((TOP_K, tm), lambda i: (0, i))
    cspec = pl.BlockSpec(counts.shape, lambda i: (0, 0))
    return pl.pallas_call(
        _outproj_kernel,
        grid=(t // tm,),
        in_specs=[rspec(d), rspec(GROUP_W), rspec(GROUP_W), _mod_spec(per_row, tm, d, rows_per_batch),
                  wspec(norm_g), wspec(w_out), wspec(rw_hi), wspec(rw_lo), wspec(rb), cspec],
        out_specs=[rspec(d), pl.BlockSpec((tm * ROW_CHUNKS, 128), lambda i: (i, 0)), tspec, tspec, tspec, cspec],
        out_shape=[jax.ShapeDtypeStruct((t, d), F32), jax.ShapeDtypeStruct((t * ROW_CHUNKS, 128), F32),
                   jax.ShapeDtypeStruct((TOP_K, t), I32), jax.ShapeDtypeStruct((TOP_K, t), F32),
                   jax.ShapeDtypeStruct((TOP_K, t), I32), jax.ShapeDtypeStruct(counts.shape, F32)],
        compiler_params=_cparams(1),
        name="outproj_norm_router",
    )(x, hm, hd, mod, norm_g, w_out, rw_hi, rw_lo, rb, counts)


def _route(top_i, rank, counts, tm, n_exp):
    t = top_i.shape[1]
    n_assign = t * TOP_K
    padded = (counts + tm - 1) // tm * tm
    pad_end = jnp.cumsum(padded)
    pad_start = pad_end - padded
    pos = (pad_start[top_i] + rank).T.reshape(-1).astype(I32)
    n_blocks = -(-(n_assign + n_exp * (tm - 1)) // tm)
    slot_t = jnp.zeros((n_blocks * tm,), I32).at[pos].set(
        jnp.arange(n_assign, dtype=I32) // TOP_K, unique_indices=True)
    block_start = jnp.arange(n_blocks, dtype=I32) * tm
    block_e = jnp.minimum(jnp.sum((pad_end[None, :] <= block_start[:, None]).astype(I32), axis=1),
                          n_exp - 1).astype(I32)
    n_used = (pad_end[-1:] // tm).astype(I32)
    return pos, slot_t, block_e, n_used, n_blocks


def _experts_kernel(slot_ref, be_ref, nu_ref, x_hbm, wgu_ref, wd_ref, bgu_ref, bd_ref,
                    y_ref, xbuf, xbf, wgu_bf, wd_tmp, wd_bf, sem, *, tm):
    i = pl.program_id(0)
    n_used = nu_ref[0]
    slot = i % GATHER_SLOTS

    new_expert = (i == 0) | (be_ref[i] != be_ref[jnp.maximum(i - 1, 0)])

    @pl.when(new_expert & (i < n_used))
    def _():
        wgu_bf[...] = wgu_ref[...].astype(BF16)
        half_f = wd_ref.shape[0] // 2
        for c in range(wd_ref.shape[1] // 128):
            cs = slice(c * 128, (c + 1) * 128)
            wd_tmp[c, pl.ds(0, half_f, stride=2), :] = wd_ref[0:half_f, cs]
            wd_tmp[c, pl.ds(1, half_f, stride=2), :] = wd_ref[half_f:, cs]
            wd_bf[:, cs] = wd_tmp[c].astype(BF16)

    def gather(blk, dst_slot):
        base = blk * tm
        for r in range(tm):
            src = pl.multiple_of(slot_ref[base + r], ROW_CHUNKS)
            pltpu.make_async_copy(x_hbm.at[pl.ds(src, ROW_CHUNKS), :],
                                  xbuf.at[dst_slot, pl.ds(r * ROW_CHUNKS, ROW_CHUNKS), :], sem.at[dst_slot]).start()

    def block(prefetch):
        pltpu.make_async_copy(x_hbm.at[pl.ds(0, tm * ROW_CHUNKS), :], xbuf.at[slot], sem.at[slot]).wait()
        xbf[...] = _from_row_tiles(xbuf.at[slot]).astype(BF16)
        if prefetch:
            gather(i + GATHER_SLOTS - 1, (i + GATHER_SLOTS - 1) % GATHER_SLOTS)
        gu = _mm(xbf[...], wgu_bf[...]) + bgu_ref[...]
        half = gu.shape[1] // 2
        lo, hi = gu[:, :half], gu[:, half:]
        even = (lax.broadcasted_iota(I32, (1, half), 1) & 1) == 0
        g = jnp.where(even, lo, pltpu.roll(hi, 1, 1))
        u = jnp.where(even, pltpu.roll(lo, half - 1, 1), hi)
        g = jnp.minimum(g, SWIGLU_LIMIT)
        u = jnp.clip(u, -SWIGLU_LIMIT, SWIGLU_LIMIT)
        act = (u + 1.0) * (g * jax.nn.sigmoid(SWIGLU_ALPHA * g))
        _to_row_tiles(y_ref, _mm(act.astype(BF16), wd_bf[...]) + bd_ref[...])

    for ahead in range(GATHER_SLOTS - 1):
        @pl.when((i == 0) & (ahead < n_used))
        def _():
            gather(ahead, ahead)

    @pl.when(i + GATHER_SLOTS - 1 < n_used)
    def _():
        block(True)

    @pl.when((i < n_used) & (i + GATHER_SLOTS - 1 >= n_used))
    def _():
        block(False)

    @pl.when(i >= n_used)
    def _():
        y_ref[...] = jnp.zeros(y_ref.shape, F32)


def _experts(l, slot_src, block_e, n_used, n_blocks, h2, wgu, wd, bgu, bd):
    tm = ROW_TILE
    wspec = lambda a: pl.BlockSpec((None, None) + a.shape[2:], lambda i, st, be, nu: (l, be[i]) + (0,) * (a.ndim - 2))
    grid_spec = pltpu.PrefetchScalarGridSpec(
        num_scalar_prefetch=3,
        grid=(n_blocks,),
        in_specs=[pl.BlockSpec(memory_space=pl.ANY), wspec(wgu), wspec(wd), wspec(bgu), wspec(bd)],
        out_specs=pl.BlockSpec((tm * ROW_CHUNKS, 128), lambda i, st, be, nu: (i, 0)),
        scratch_shapes=[pltpu.VMEM((GATHER_SLOTS, tm * ROW_CHUNKS, 128), F32),
                        pltpu.VMEM((tm, ROW_CHUNKS * 128), BF16),
                        pltpu.VMEM(wgu.shape[2:], BF16), pltpu.VMEM((wd.shape[3] // 128, wd.shape[2], 128), F32),
                        pltpu.VMEM(wd.shape[2:], BF16), pltpu.SemaphoreType.DMA((GATHER_SLOTS,))],
    )
    return pl.pallas_call(
        functools.partial(_experts_kernel, tm=tm),
        grid_spec=grid_spec,
        out_shape=jax.ShapeDtypeStruct((n_blocks * tm * ROW_CHUNKS, 128), F32),
        compiler_params=_cparams(1, vmem=EXPERTS_VMEM_LIMIT),
        name="experts",
    )(slot_src, block_e, n_used, h2, wgu, wd, bgu, bd)


def _combine_kernel(pos_ref, x_ref, mod_ref, gate_ref, y_hbm, fg_ref, o_ref, buf, sem, *, tm, final):
    i = pl.program_id(0)

    def gather(blk, slot):
        def body(r, carry):
            dst = pl.multiple_of(r * ROW_CHUNKS, ROW_CHUNKS)
            for k in range(TOP_K):
                src = pl.multiple_of(pos_ref[(blk * tm + r) * TOP_K + k], ROW_CHUNKS)
                pltpu.make_async_copy(y_hbm.at[pl.ds(src, ROW_CHUNKS), :],
                                      buf.at[slot, k, pl.ds(dst, ROW_CHUNKS), :], sem.at[slot]).start()
            return carry
        lax.fori_loop(0, tm, body, 0, unroll=4)

    @pl.when(i == 0)
    def _():
        gather(0, 0)

    slot = i % 2

    @pl.when(i + 1 < pl.num_programs(0))
    def _():
        gather(i + 1, 1 - slot)

    for k in range(TOP_K):
        pltpu.make_async_copy(y_hbm.at[pl.ds(0, tm * ROW_CHUNKS), :], buf.at[slot, k], sem.at[slot]).wait()
    gate = gate_ref[...]
    y = gate[:, 0:1] * _from_row_tiles(buf.at[slot, 0])
    for k in range(1, TOP_K):
        y = y + gate[:, k:k + 1] * _from_row_tiles(buf.at[slot, k])
    out = x_ref[...] + mod_ref[5] * y
    o_ref[...] = _rms(out, fg_ref[...]) if final else out


def _combine(pos, x, mod, per_row, rows_per_batch, gate, ys, final_g, final):
    t, d = x.shape
    tm = ROW_TILE
    grid_spec = pltpu.PrefetchScalarGridSpec(
        num_scalar_prefetch=1,
        grid=(t // tm,),
        in_specs=[pl.BlockSpec((tm, d), lambda i, p: (i, 0)),
                  _wrap_prefetch(_mod_spec(per_row, tm, d, rows_per_batch)),
                  pl.BlockSpec((tm, TOP_K), lambda i, p: (i, 0)),
                  pl.BlockSpec(memory_space=pl.ANY),
                  pl.BlockSpec((1, d), lambda i, p: (0, 0))],
        out_specs=pl.BlockSpec((tm, d), lambda i, p: (i, 0)),
        scratch_shapes=[pltpu.VMEM((2, TOP_K, tm * ROW_CHUNKS, 128), F32), pltpu.SemaphoreType.DMA((2,))],
    )
    return pl.pallas_call(
        functools.partial(_combine_kernel, tm=tm, final=final),
        grid_spec=grid_spec,
        out_shape=jax.ShapeDtypeStruct((t, d), F32),
        compiler_params=_cparams(1),
        name="combine",
    )(pos, x, mod, gate, ys, final_g)


def _wrap_prefetch(spec):
    return pl.BlockSpec(spec.block_shape, lambda i, p: spec.index_map(i))


def kernel(x_prompt, x_sample, c_prompt, c_sample, cache_k, cache_v, state_C, state_n, state_m, page_table, norm1_g, ada_w, ada_b, w_in, b_igate, b_fgate, mlstm_norm_g, lam_q1, lam_k1, lam_q2, lam_k2, diff_norm_g, w_out, norm2_g, router_w, router_b, w_gu, b_gu, w_down, b_down, final_norm_g):
    n_layers = ada_w.shape[0]
    bp, seq, d = x_prompt.shape
    bd, n_tok, _ = x_sample.shape
    n_exp = router_w.shape[-1]
    tp, ts = bp * seq, bd * n_tok
    nqk = N_HEADS * HEAD_QK
    assert d == 2 * GROUP_W and n_tok == 8 and seq % MLSTM_CHUNK == 0 and ts % ROW_TILE == 0
    assert w_in.shape[-1] == 2 * nqk + GROUP_W + 2 * N_HEADS + 4 * GROUP_W

    g_lo = 2 * nqk + GROUP_W
    wa = w_in[:, :, :g_lo].astype(BF16)
    wb = w_in[:, :, g_lo + 2 * N_HEADS:].astype(BF16)
    w_gate = w_in[:, :, g_lo:g_lo + 2 * N_HEADS]
    wg_col = jnp.pad(w_gate, ((0, 0), (0, 0), (0, 128 - 2 * N_HEADS))).astype(BF16)
    wg_row = jnp.swapaxes(w_gate, 1, 2).astype(BF16)
    b_gate = jnp.concatenate([b_igate, b_fgate], axis=1).astype(F32)
    bg_col = jnp.pad(b_gate, ((0, 0), (0, 128 - 2 * N_HEADS)))[:, None, :]
    bg_row = b_gate[:, :, None]
    w_out_bf = w_out.astype(BF16)
    rw_t = jnp.swapaxes(router_w, 1, 2).astype(F32)
    rw_hi = rw_t.astype(BF16)
    rw_lo = (rw_t - rw_hi.astype(F32)).astype(BF16)
    rb = router_b.astype(F32)[:, :, None]
    b_gu_f = b_gu[..., None, :].astype(F32)
    b_d = b_down[..., None, :].astype(F32)
    n1g, n2g = norm1_g[:, None, :], norm2_g[:, None, :]
    mng, dng = mlstm_norm_g[:, None, :], diff_norm_g[:, None, :]
    fng = final_norm_g[None, :]
    slopes = jnp.asarray(2.0 ** (-8.0 * np.arange(1, N_HEADS + 1) / N_HEADS), F32)
    assert cache_k.shape[3:] == (N_HEADS, HEAD_V) and cache_v.shape[3:] == (N_HEADS, HEAD_V)
    pool, page = cache_k.shape[1], cache_k.shape[2]
    ck = cache_k.reshape(n_layers, pool, page * N_HEADS, HEAD_V)
    cv = cache_v.reshape(n_layers, pool, page * N_HEADS, HEAD_V)
    cinfo = _paged_row_info(n_tok)

    mod_all = _adaln_mod(jnp.concatenate([c_prompt, c_sample], axis=0), ada_w, ada_b)

    xp = x_prompt.reshape(tp, d)
    xs = x_sample.reshape(ts, d)
    zero_state = _pad_state(jnp.zeros((bp, N_HEADS, HEAD_V, HEAD_QK), F32), jnp.zeros((bp, N_HEADS, HEAD_QK), F32),
                            jnp.zeros((bp, N_HEADS), F32))
    state_s = [a.reshape((n_layers, bd) + a.shape[1:]) for a in _pad_state(
        state_C.reshape((n_layers * bd,) + state_C.shape[2:]), state_n.reshape((n_layers * bd,) + state_n.shape[2:]),
        state_m.reshape((n_layers * bd,) + state_m.shape[2:]))]
    n_chunks = seq // MLSTM_CHUNK
    no_counts = jnp.zeros((n_exp, 128), F32)
    outs = []
    for l in range(n_layers):
        lam_init = 0.8 - 0.6 * math.exp(-0.3 * l)
        lam = (jnp.exp(jnp.sum(lam_q1[l].astype(F32) * lam_k1[l].astype(F32)))
               - jnp.exp(jnp.sum(lam_q2[l].astype(F32) * lam_k2[l].astype(F32))) + lam_init)
        scalars = jnp.concatenate([lam[None], slopes]).astype(F32)
        mod_p = mod_all[l, :bp].reshape(bp, 6, 1, d)
        mod_s = jnp.repeat(mod_all[l, bp:].reshape(bd, 6, d).transpose(1, 0, 2), n_tok, axis=1)
        proj_w = (n1g, wa, wb, wg_col, wg_row, bg_col, bg_row)

        qk, vm, og, qd, kd_p, vd_p, gcol, grow = _inproj(l, xp, mod_p, False, seq, *proj_w)
        grow = grow.reshape(tp // ROW_TILE, 2 * N_HEADS, ROW_TILE // MLSTM_CHUNK, MLSTM_CHUNK)
        grow = grow.transpose(0, 2, 1, 3).reshape(tp // MLSTM_CHUNK, 2 * N_HEADS, MLSTM_CHUNK)
        hm_p, c_p, n_p, m_p = _mlstm(l, qk, vm, og, gcol, grow, *zero_state, mng, bp, n_chunks, MLSTM_CHUNK)
        hd_p = _attn_prompt(l, scalars, qd, kd_p, vd_p, dng, bp, seq, 1.0 - lam_init)

        qk, vm, og, qd, kd_s, vd_s, gcol, grow = _inproj(l, xs, mod_s, True, 0, *proj_w)
        pad_rows = lambda a: jnp.pad(a.reshape(bd, n_tok, a.shape[-1]),
                                     ((0, 0), (0, MLSTM_PAD - n_tok), (0, 0)))
        gcol = gcol.reshape(bd, n_tok, 128)
        gcol = jnp.concatenate(
            [gcol, jnp.broadcast_to(jnp.where(jnp.arange(128) < N_HEADS, NEG_BIG, 0.0).astype(F32),
                                    (bd, MLSTM_PAD - n_tok, 128))], axis=1).reshape(bd * MLSTM_PAD, 128)
        grow = grow.transpose(1, 0, 2).reshape(2 * N_HEADS, bd, n_tok).transpose(1, 0, 2)
        grow = jnp.concatenate(
            [grow, jnp.broadcast_to(jnp.where(jnp.arange(2 * N_HEADS) < N_HEADS, NEG_BIG, 0.0).astype(F32)[None, :, None],
                                    (bd, 2 * N_HEADS, MLSTM_PAD - n_tok))], axis=2)
        flat = lambda a: pad_rows(a).reshape(bd * MLSTM_PAD, a.shape[-1])
        hm_s, c_s, n_s, m_s = _mlstm(l, flat(qk), flat(vm), flat(og), gcol, grow,
                                     *[a[l] for a in state_s], mng, bd, 1, MLSTM_PAD)
        hm_s = hm_s.reshape(bd, MLSTM_PAD, GROUP_W)[:, :n_tok].reshape(ts, GROUP_W)
        wt = _paged_query_matrix(qd, bd, n_tok)
        hd_s = _attn_paged(l, page_table, scalars, wt, ck, cv, pad_rows(kd_s), pad_rows(vd_s), cinfo, dng,
                           1.0 - lam_init)
        hd_s = hd_s.reshape(ts, GROUP_W).astype(BF16)

        post_w = (n2g, w_out_bf, rw_hi, rw_lo, rb)
        xp, h2_p, ti_p, tg_p, rk_p, counts = _outproj(l, xp, hm_p, hd_p, mod_p, False, seq, *post_w, no_counts)
        xs, h2_s, ti_s, tg_s, rk_s, counts = _outproj(l, xs, hm_s, hd_s, mod_s, True, 0, *post_w, counts)
        h2 = jnp.concatenate([h2_p, h2_s], axis=0)
        pos, slot_t, block_e, n_used, n_blocks = _route(
            jnp.concatenate([ti_p, ti_s], axis=1), jnp.concatenate([rk_p, rk_s], axis=1),
            counts[:, 0].astype(I32), ROW_TILE, n_exp)
        ys = _experts(l, slot_t * ROW_CHUNKS, block_e, n_used, n_blocks, h2, w_gu, w_down, b_gu_f, b_d)
        final = l == n_layers - 1
        pos = pos * ROW_CHUNKS
        xp = _combine(pos[:tp * TOP_K], xp, mod_p, False, seq, tg_p.T, ys, fng, final)
        xs = _combine(pos[tp * TOP_K:], xs, mod_s, True, 0, tg_s.T, ys, fng, final)

        outs.append((kd_p, vd_p, c_p, n_p, m_p, kd_s, vd_s, c_s, n_s, m_s))

    def stacked(i, shape, dtype):
        return jnp.stack([o[i] for o in outs]).reshape((n_layers,) + shape).astype(dtype)

    sdt = state_C.dtype

    def final_states(first, n_seq):
        flat = _unpad_state(*[jnp.concatenate([o[first + i] for o in outs], axis=0) for i in range(3)])
        return [a.reshape((n_layers, n_seq) + a.shape[1:]).astype(sdt) for a in flat]

    states_p, states_s = final_states(2, bp), final_states(7, bd)
    st = lambda states, i: states[i]
    return (xp.reshape(bp, seq, d), xs.reshape(bd, n_tok, d),
            stacked(0, (bp, seq, N_HEADS, 2 * HEAD_QK), cache_k.dtype),
            stacked(1, (bp, seq, N_HEADS, HEAD_V), cache_v.dtype),
            st(states_p, 0), st(states_p, 1), st(states_p, 2),
            stacked(5, (bd, n_tok, N_HEADS, 2 * HEAD_QK), cache_k.dtype),
            stacked(6, (bd, n_tok, N_HEADS, HEAD_V), cache_v.dtype),
            st(states_s, 0), st(states_s, 1), st(states_s, 2))
```

```python
import functools
import math

import numpy as np
import jax
import jax.numpy as jnp
from jax import lax
from jax.experimental import pallas as pl
from jax.experimental.pallas import tpu as pltpu

F32 = jnp.float32
BF16 = jnp.bfloat16
I32 = jnp.int32

NORM_EPS = 1e-6
N_HEADS = 4
HEAD_V = 128
HEAD_QK = 64
GROUP_W = N_HEADS * HEAD_V
TOP_K = 4
SWIGLU_LIMIT = 7.0
SWIGLU_ALPHA = 1.702
NEG_BIG = -1e30

ROW_TILE = 256
MLSTM_CHUNK = 256
MLSTM_PAD = 128
ATTN_TILE = 256
ATTN_CHUNK = 128
PAGES_PER_STEP = 8
GATHER_SLOTS = 3
VMEM_LIMIT = 48 * 1024 * 1024
EXPERTS_VMEM_LIMIT = 56 * 1024 * 1024


def _cparams(n_axes, vmem=VMEM_LIMIT):
    return pltpu.CompilerParams(dimension_semantics=("arbitrary",) * n_axes, vmem_limit_bytes=vmem)


def _nt(a, b):
    return lax.dot_general(a, b, (((1,), (1,)), ((), ())), preferred_element_type=F32)


def _mm(a, b):
    return jnp.dot(a, b, preferred_element_type=F32)


def _split3(x):
    hi = x.astype(BF16)
    r1 = x - hi.astype(F32)
    mid = r1.astype(BF16)
    lo = (r1 - mid.astype(F32)).astype(BF16)
    return hi, mid, lo


def _log_sigmoid(x):
    return jnp.minimum(x, 0.0) - jnp.log1p(jnp.exp(-jnp.abs(x)))


def _rms(x, g):
    return x * lax.rsqrt(jnp.mean(x * x, axis=-1, keepdims=True) + NORM_EPS) * g


ROW_CHUNKS = 8


def _to_row_tiles(ref, x):
    rows, c_n = x.shape[0], x.shape[1] // 128
    for c in range(c_n):
        ref[pl.ds(c, rows, stride=c_n), :] = x[:, c * 128:(c + 1) * 128]


def _from_row_tiles(ref, c_n=ROW_CHUNKS):
    rows = ref.shape[0] // c_n
    return jnp.concatenate([ref[pl.ds(c, rows, stride=c_n), :] for c in range(c_n)], axis=1)


def _mod_kernel(c_ref, w_ref, b_ref, o_ref):
    c = c_ref[...]
    s = c * jax.nn.sigmoid(c)
    o_ref[...] = _mm(s.astype(BF16), w_ref[...].astype(BF16)) + b_ref[...]


def _adaln_mod(c_all, ada_w, ada_b):
    n_layers, d, n = ada_w.shape
    bc = c_all.shape[0]
    tn = n // 4
    return pl.pallas_call(
        _mod_kernel,
        grid=(n_layers, n // tn),
        in_specs=[pl.BlockSpec((bc, d), lambda l, j: (0, 0)),
                  pl.BlockSpec((None, d, tn), lambda l, j: (l, 0, j)),
                  pl.BlockSpec((None, 1, tn), lambda l, j: (l, 0, j))],
        out_specs=pl.BlockSpec((None, bc, tn), lambda l, j: (l, 0, j)),
        out_shape=jax.ShapeDtypeStruct((n_layers, bc, n), F32),
        compiler_params=_cparams(2),
        name="adaln_mod",
    )(c_all, ada_w, ada_b.reshape(n_layers, 1, n))


def _mod_spec(per_row, tm, d, rows_per_batch):
    if per_row:
        return pl.BlockSpec((6, tm, d), lambda i: (0, i, 0))
    tiles_per_batch = rows_per_batch // tm
    return pl.BlockSpec((None, 6, 1, d), lambda i: (i // tiles_per_batch, 0, 0, 0))


def _inproj_kernel(x_ref, mod_ref, g_ref, wa_ref, wb_ref, wg_ref, wgt_ref, bgc_ref, bgr_ref,
                   qk_ref, vm_ref, og_ref, qd_ref, kd_ref, vd_ref, gcol_ref, grow_ref):
    h = (_rms(x_ref[...], g_ref[...]) * (1.0 + mod_ref[1]) + mod_ref[0]).astype(BF16)
    a = _mm(h, wa_ref[...])
    lane = lax.broadcasted_iota(I32, (1, GROUP_W), 1)
    k_scale = jnp.where(lane >= N_HEADS * HEAD_QK, HEAD_QK ** -0.5, 1.0)
    qk_ref[...] = (a[:, :GROUP_W] * k_scale).astype(BF16)
    vm_ref[...] = a[:, GROUP_W:].astype(BF16)
    b = _mm(h, wb_ref[...])
    og_ref[...] = jax.nn.sigmoid(b[:, :GROUP_W]).astype(BF16)
    qd_ref[...] = (b[:, GROUP_W:2 * GROUP_W] * HEAD_QK ** -0.5).astype(BF16)
    kd_ref[...] = b[:, 2 * GROUP_W:3 * GROUP_W]
    vd_ref[...] = b[:, 3 * GROUP_W:]
    gc = _mm(h, wg_ref[...]) + bgc_ref[...]
    lane_g = lax.broadcasted_iota(I32, gc.shape, 1)
    gcol_ref[...] = jnp.where((lane_g >= N_HEADS) & (lane_g < 2 * N_HEADS), _log_sigmoid(gc), gc)
    gr = _nt(wgt_ref[...], h) + bgr_ref[...]
    row_g = lax.broadcasted_iota(I32, gr.shape, 0)
    grow_ref[...] = jnp.where(row_g >= N_HEADS, _log_sigmoid(gr), gr)


def _inproj(l, x, mod, per_row, rows_per_batch, norm_g, wa, wb, wg, wgt, bgc, bgr):
    t, d = x.shape
    tm = ROW_TILE
    nt = t // tm
    row = lambda w, dt: jax.ShapeDtypeStruct((t, w), dt)
    rspec = lambda w: pl.BlockSpec((tm, w), lambda i: (i, 0))
    wspec = lambda a: pl.BlockSpec((None,) + a.shape[1:], lambda i: (l,) + (0,) * (a.ndim - 1))
    return pl.pallas_call(
        _inproj_kernel,
        grid=(nt,),
        in_specs=[rspec(d), _mod_spec(per_row, tm, d, rows_per_batch), wspec(norm_g),
                  wspec(wa), wspec(wb), wspec(wg), wspec(wgt), wspec(bgc), wspec(bgr)],
        out_specs=[rspec(GROUP_W)] * 6 + [rspec(128), pl.BlockSpec((None, 2 * N_HEADS, tm), lambda i: (i, 0, 0))],
        out_shape=[row(GROUP_W, BF16)] * 4 + [row(GROUP_W, F32)] * 2
                  + [row(128, F32), jax.ShapeDtypeStruct((nt, 2 * N_HEADS, tm), F32)],
        compiler_params=_cparams(1),
        name="norm_inproj",
    )(x, mod, norm_g, wa, wb, wg, wgt, bgc, bgr)


def _mlstm_kernel(qk_ref, v_ref, og_ref, gc_ref, gr_ref, c0_ref, n0_ref, m0_ref, g_ref,
                  h_ref, c_ref, n_ref, m_ref, *, chunk):
    @pl.when(pl.program_id(1) == 0)
    def _():
        c_ref[...] = c0_ref[...]
        n_ref[...] = n0_ref[...]
        m_ref[...] = m0_ref[...]

    gc = gc_ref[...]
    gr = gr_ref[...]
    r = lax.broadcasted_iota(I32, (chunk, chunk), 0)
    s = lax.broadcasted_iota(I32, (chunk, chunk), 1)
    causal = r >= s
    tri = causal.astype(BF16)
    tri_t = (r <= s).astype(BF16)
    lane_g = lax.broadcasted_iota(I32, gc.shape, 1)
    gl = jnp.where(lane_g >= N_HEADS, gc, 0.0)
    cum_c = sum(_mm(tri, p) for p in _split3(gl))
    cum_r = sum(_mm(p, tri_t) for p in _split3(gr))
    lane = lax.broadcasted_iota(I32, (1, HEAD_V), 1)

    for h in range(N_HEADS):
        t0 = (h // 2) * HEAD_V
        off = (h % 2) * HEAD_QK
        hmask = (lane >= off) & (lane < off + HEAD_QK)
        zero = jnp.zeros((), BF16)
        qh = jnp.where(hmask, qk_ref[:, t0:t0 + HEAD_V], zero)
        kh = jnp.where(hmask, qk_ref[:, N_HEADS * HEAD_QK + t0:N_HEADS * HEAD_QK + t0 + HEAD_V], zero)
        hs = slice(h * HEAD_V, (h + 1) * HEAD_V)
        vh = v_ref[:, hs]
        ig_r = gr[h:h + 1, :]
        b_r = cum_r[N_HEADS + h:N_HEADS + h + 1, :]
        ig_c = gc[:, h:h + 1]
        b_c = cum_c[:, N_HEADS + h:N_HEADS + h + 1]
        m_prev = m_ref[h:h + 1, 0:1]
        c_prev = c_ref[h]
        n_prev = n_ref[h:h + 1, :]

        dmat = jnp.where(causal, b_c - b_r + ig_r, -jnp.inf)
        inter = b_c + m_prev
        m_row = jnp.maximum(jnp.max(dmat, axis=1, keepdims=True), inter)
        w_intra = _nt(qh, kh) * jnp.exp(dmat - m_row)
        w_inter = jnp.exp(inter - m_row)
        num = _mm(w_intra.astype(BF16), vh) + w_inter * _nt(qh, c_prev.astype(BF16))
        den = (jnp.sum(w_intra, axis=1, keepdims=True)
               + w_inter * jnp.sum(qh.astype(F32) * n_prev, axis=1, keepdims=True))
        hv = num / jnp.maximum(jnp.abs(den), jnp.exp(-m_row))
        h_ref[:, hs] = (_rms(hv, g_ref[:, hs]) * og_ref[:, hs].astype(F32)).astype(BF16)

        b_end = b_c[chunk - 1:chunk, :]
        w_end = b_end - b_r + ig_r
        m_new = jnp.maximum(b_end + m_prev, jnp.max(w_end, axis=1, keepdims=True))
        decay = jnp.exp(b_end + m_prev - m_new)
        kw = kh.astype(F32) * jnp.exp(b_end - b_c + ig_c - m_new)
        v_t = vh.astype(F32).T.astype(BF16)
        c_ref[h] = decay * c_prev + _mm(v_t, kw.astype(BF16))
        n_ref[h:h + 1, :] = decay * n_prev + jnp.sum(kw, axis=0, keepdims=True)
        m_ref[h:h + 1, :] = jnp.broadcast_to(m_new, (1, HEAD_V))


def _mlstm(l, qk, vm, og, gcol, grow, c0, n0, m0, norm_g, n_seq, n_chunks, chunk):
    t = qk.shape[0]
    rspec = lambda w: pl.BlockSpec((chunk, w), lambda b, c: (b * n_chunks + c, 0))
    sspec = lambda shp: pl.BlockSpec((None,) + shp, lambda b, c: (b,) + (0,) * len(shp))
    c_shape, s_shape = (N_HEADS, HEAD_V, HEAD_V), (2 * N_HEADS, HEAD_V)
    return pl.pallas_call(
        functools.partial(_mlstm_kernel, chunk=chunk),
        grid=(n_seq, n_chunks),
        in_specs=[rspec(GROUP_W), rspec(GROUP_W), rspec(GROUP_W), rspec(128),
                  pl.BlockSpec((None, 2 * N_HEADS, chunk), lambda b, c: (b * n_chunks + c, 0, 0)),
                  sspec(c_shape), sspec(s_shape), sspec(s_shape),
                  pl.BlockSpec((None, 1, GROUP_W), lambda b, c: (l, 0, 0))],
        out_specs=[rspec(GROUP_W), sspec(c_shape), sspec(s_shape), sspec(s_shape)],
        out_shape=[jax.ShapeDtypeStruct((t, GROUP_W), BF16),
                   jax.ShapeDtypeStruct((n_seq,) + c_shape, F32),
                   jax.ShapeDtypeStruct((n_seq,) + s_shape, F32),
                   jax.ShapeDtypeStruct((n_seq,) + s_shape, F32)],
        compiler_params=_cparams(2),
        name="mlstm",
    )(qk, vm, og, gcol, grow, c0, n0, m0, norm_g)


def _pad_state(c, n, m):
    b = c.shape[0]
    c_pad = jnp.zeros((b, N_HEADS, HEAD_V, HEAD_V), F32)
    n_pad = jnp.zeros((b, 2 * N_HEADS, HEAD_V), F32)
    for h in range(N_HEADS):
        off = (h % 2) * HEAD_QK
        c_pad = c_pad.at[:, h, :, off:off + HEAD_QK].set(c[:, h].astype(F32))
        n_pad = n_pad.at[:, h, off:off + HEAD_QK].set(n[:, h].astype(F32))
    m_pad = jnp.zeros((b, 2 * N_HEADS, HEAD_V), F32).at[:, :N_HEADS, :].set(
        jnp.broadcast_to(m.astype(F32)[:, :, None], (b, N_HEADS, HEAD_V)))
    return c_pad, n_pad, m_pad


def _unpad_state(c_pad, n_pad, m_pad):
    c = jnp.stack([c_pad[:, h, :, (h % 2) * HEAD_QK:(h % 2) * HEAD_QK + HEAD_QK] for h in range(N_HEADS)], 1)
    n = jnp.stack([n_pad[:, h, (h % 2) * HEAD_QK:(h % 2) * HEAD_QK + HEAD_QK] for h in range(N_HEADS)], 1)
    return c, n, m_pad[:, :N_HEADS, 0]


def _head_finish(o1, o2, lam, g, out_scale):
    d = o1 - lam * o2
    return _rms(d, g) * out_scale


def _attn_prompt_kernel(sc_ref, q_ref, k_ref, v_ref, g_ref, o_ref, kb_ref, vb_ref, q2_ref, *state,
                        tile, chunk, out_scale):
    h = pl.program_id(1)
    qi = pl.program_id(2)
    n_chunks = 2 * tile // chunk
    m_refs, l_refs, acc_refs = state[0::3], state[1::3], state[2::3]

    @pl.when(qi == 0)
    def _():
        kb_ref[...] = k_ref[...].astype(BF16)
        vb_ref[...] = v_ref[...].astype(BF16)

    lam = sc_ref[0]
    slope = sc_ref[1 + h]
    q = q_ref[...]
    lane = lax.broadcasted_iota(I32, (1, HEAD_V), 1)
    zero = jnp.zeros((), BF16)
    q2_ref[0:tile, :] = jnp.where(lane < HEAD_QK, q, zero)
    q2_ref[tile:2 * tile, :] = jnp.where(lane >= HEAD_QK, q, zero)
    for c in range(n_chunks):
        m_refs[c][...] = jnp.full((chunk, HEAD_V), -jnp.inf, F32)
        l_refs[c][...] = jnp.zeros((chunk, HEAD_V), F32)
        acc_refs[c][...] = jnp.zeros((chunk, HEAD_V), F32)
    col = lax.broadcasted_iota(I32, (1, tile), 1)
    ones = jnp.ones((tile, HEAD_V), BF16)

    def qk(j):
        ks = kb_ref[pl.ds(pl.multiple_of(j * tile, tile), tile), :]
        return tuple(_nt(q2_ref[c * chunk:(c + 1) * chunk, :], ks) for c in range(n_chunks))

    def step(j, scores, diagonal):
        v_ones = jnp.concatenate([vb_ref[pl.ds(pl.multiple_of(j * tile, tile), tile), :], ones], axis=1)
        bias = slope * (col + (j - qi) * tile).astype(F32)
        for c in range(n_chunks):
            sc = scores[c] + bias
            if diagonal:
                row = lax.broadcasted_iota(I32, (chunk, 1), 0) + (c * chunk) % tile
                sc = jnp.where(col <= row, sc, -jnp.inf)
            m_old = m_refs[c][...]
            m_new = jnp.maximum(m_old, jnp.max(sc, axis=1, keepdims=True))
            alpha = jnp.exp(m_old - m_new)
            p = jnp.exp(sc - jnp.concatenate([m_new] * (tile // HEAD_V), axis=1)).astype(BF16)
            pv = _mm(p, v_ones)
            acc_refs[c][...] = alpha * acc_refs[c][...] + pv[:, :HEAD_V]
            l_refs[c][...] = alpha * l_refs[c][...] + pv[:, HEAD_V:]
            m_refs[c][...] = m_new

    def body(j, scores):
        nxt = qk(j + 1)
        step(j, scores, False)
        return nxt

    scores = lax.fori_loop(0, qi, body, qk(0))
    step(qi, scores, True)
    o = jnp.concatenate([acc_refs[c][...] / l_refs[c][...] for c in range(n_chunks)], axis=0)
    o_ref[...] = _head_finish(o[:tile], o[tile:], lam, g_ref[...], out_scale).astype(BF16)


def _attn_prompt(l, scalars, qd, kd, vd, norm_g, n_seq, seq, out_scale):
    t = qd.shape[0]
    tile = min(ATTN_TILE, seq)
    chunk = min(ATTN_CHUNK, tile)
    nq = seq // tile
    return pl.pallas_call(
        functools.partial(_attn_prompt_kernel, tile=tile, chunk=chunk, out_scale=out_scale),
        grid=(n_seq, N_HEADS, nq),
        in_specs=[pl.BlockSpec(memory_space=pltpu.SMEM),
                  pl.BlockSpec((tile, HEAD_V), lambda b, h, i: (b * nq + i, h)),
                  pl.BlockSpec((seq, HEAD_V), lambda b, h, i: (b, h)),
                  pl.BlockSpec((seq, HEAD_V), lambda b, h, i: (b, h)),
                  pl.BlockSpec((None, 1, HEAD_V), lambda b, h, i: (l, 0, 0))],
        out_specs=pl.BlockSpec((tile, HEAD_V), lambda b, h, i: (b * nq + i, h)),
        out_shape=jax.ShapeDtypeStruct((t, GROUP_W), BF16),
        scratch_shapes=[pltpu.VMEM((seq, HEAD_V), BF16), pltpu.VMEM((seq, HEAD_V), BF16),
                        pltpu.VMEM((2 * tile, HEAD_V), BF16)]
                       + [pltpu.VMEM((chunk, HEAD_V), F32)] * (3 * (2 * tile // chunk)),
        compiler_params=_cparams(3),
        name="attn_prompt",
    )(scalars, qd, kd, vd, norm_g)


def _attn_paged_kernel(pt_ref, sc_ref, wt_ref, *refs, n_pages, past, out_scale):
    del pt_ref
    k_refs, v_refs = refs[:n_pages], refs[n_pages:2 * n_pages]
    kn_ref, vn_ref, ci_ref, g_ref, o_ref, m_ref, l_ref, acc_ref = refs[2 * n_pages:]
    s = pl.program_id(1)
    n_keys = n_pages * k_refs[0].shape[0] // N_HEADS

    @pl.when(s == 0)
    def _():
        m_ref[...] = jnp.full(m_ref.shape, -jnp.inf, F32)
        l_ref[...] = jnp.zeros(l_ref.shape, F32)
        acc_ref[...] = jnp.zeros(acc_ref.shape, F32)

    wt = wt_ref[...]
    slope = ci_ref[:, 0:1]
    tok = ci_ref[:, 1:2]

    def update(sc, v_bf):
        m_old = m_ref[...]
        m_new = jnp.maximum(m_old, jnp.max(sc, axis=1, keepdims=True))
        alpha = jnp.exp(m_old - m_new)
        p = jnp.exp(sc - m_new)
        l_ref[...] = alpha * l_ref[...] + jnp.sum(p, axis=1, keepdims=True)
        acc_ref[...] = alpha * acc_ref[...] + _mm(p.astype(BF16), v_bf)
        m_ref[...] = m_new

    def pages(page_refs):
        return jnp.concatenate([_from_row_tiles(r, N_HEADS) for r in page_refs], axis=0).astype(BF16)

    k_bf = pages(k_refs)
    v_bf = pages(v_refs)
    kpos = lax.broadcasted_iota(I32, (1, n_keys), 1) + (s * n_keys - past)
    update(_nt(wt, k_bf) + slope * (kpos.astype(F32) - tok), v_bf)

    @pl.when(s == pl.num_programs(1) - 1)
    def _():
        r = lax.broadcasted_iota(I32, (1, kn_ref.shape[0]), 1).astype(F32)
        sc = _nt(wt, kn_ref[...].astype(BF16)) + slope * (r - tok)
        update(jnp.where(r <= tok, sc, -jnp.inf), vn_ref[...].astype(BF16))
        o = acc_ref[...] / l_ref[...]
        lam = sc_ref[0]
        n_tok = o_ref.shape[0]
        for h in range(N_HEADS):
            hs = slice(h * HEAD_V, (h + 1) * HEAD_V)
            c0 = h * 2 * n_tok
            o_ref[:, hs] = _head_finish(o[c0:c0 + n_tok, hs], o[c0 + n_tok:c0 + 2 * n_tok, hs],
                                        lam, g_ref[...], out_scale)


def _attn_paged(l, page_table, scalars, wt, cache_k, cache_v, k_new, v_new, cinfo, norm_g, out_scale):
    n_seq, pages_per_seq = page_table.shape
    page = cache_k.shape[2] // N_HEADS
    n_pages = min(PAGES_PER_STEP, pages_per_seq)
    n_steps = pages_per_seq // n_pages
    n_tok = 8

    def page_spec(i):
        return pl.BlockSpec((None, None, page * N_HEADS, HEAD_V),
                            lambda b, s, pt: (l, pt[b, s * n_pages + i], 0, 0))

    seq_spec = lambda shp: pl.BlockSpec((None,) + shp, lambda b, s, pt: (b,) + (0,) * len(shp))
    grid_spec = pltpu.PrefetchScalarGridSpec(
        num_scalar_prefetch=1,
        grid=(n_seq, n_steps),
        in_specs=[pl.BlockSpec(memory_space=pltpu.SMEM), seq_spec((128, GROUP_W))]
                 + [page_spec(i) for i in range(n_pages)] * 2
                 + [seq_spec((128, GROUP_W)), seq_spec((128, GROUP_W)),
                    pl.BlockSpec((128, 128), lambda b, s, pt: (0, 0)),
                    pl.BlockSpec((None, 1, HEAD_V), lambda b, s, pt: (l, 0, 0))],
        out_specs=seq_spec((n_tok, GROUP_W)),
        scratch_shapes=[pltpu.VMEM((128, 1), F32), pltpu.VMEM((128, 1), F32), pltpu.VMEM((128, GROUP_W), F32)],
    )
    return pl.pallas_call(
        functools.partial(_attn_paged_kernel, n_pages=n_pages, past=pages_per_seq * page, out_scale=out_scale),
        grid_spec=grid_spec,
        out_shape=jax.ShapeDtypeStruct((n_seq, n_tok, GROUP_W), F32),
        compiler_params=_cparams(2),
        name="attn_paged",
    )(page_table, scalars, wt, *([cache_k] * n_pages), *([cache_v] * n_pages), k_new, v_new, cinfo, norm_g)


def _paged_query_matrix(qd, n_seq, n_tok):
    q5 = qd.reshape(n_seq, n_tok, N_HEADS, 2, HEAD_QK).transpose(0, 2, 3, 1, 4)
    eye_h = jnp.eye(N_HEADS, dtype=qd.dtype)
    eye_j = jnp.eye(2, dtype=qd.dtype)
    wt = jnp.einsum('bhjtd,hg,ji->bhjtgid', q5, eye_h, eye_j)
    wt = wt.reshape(n_seq, N_HEADS * 2 * n_tok, GROUP_W)
    return jnp.pad(wt, ((0, 0), (0, 128 - wt.shape[1]), (0, 0)))


def _paged_row_info(n_tok):
    info = np.zeros((128, 128), np.float32)
    for h in range(N_HEADS):
        for j in range(2):
            for tkn in range(n_tok):
                c = h * 2 * n_tok + j * n_tok + tkn
                info[c, 0] = 2.0 ** (-8.0 * (h + 1) / N_HEADS)
                info[c, 1] = tkn
    return jnp.asarray(info)


def _outproj_kernel(x_ref, hm_ref, hd_ref, mod_ref, g_ref, w_ref, rwh_ref, rwl_ref, rb_ref, cnt0_ref,
                    xo_ref, h2_ref, ti_ref, tg_ref, rk_ref, cnt_ref):
    @pl.when(pl.program_id(0) == 0)
    def _():
        cnt_ref[...] = cnt0_ref[...]

    mix = jnp.concatenate([hm_ref[...], hd_ref[...]], axis=1)
    xn = x_ref[...] + mod_ref[2] * _mm(mix, w_ref[...])
    xo_ref[...] = xn
    h2 = _rms(xn, g_ref[...]) * (1.0 + mod_ref[4]) + mod_ref[3]
    _to_row_tiles(h2_ref, h2)
    hh = h2.astype(BF16)
    hl = (h2 - hh.astype(F32)).astype(BF16)
    lg = _nt(rwh_ref[...], hh) + _nt(rwh_ref[...], hl) + _nt(rwl_ref[...], hh) + rb_ref[...]
    n_exp = lg.shape[0]
    eidx = lax.broadcasted_iota(I32, lg.shape, 0).astype(F32)
    vals, ids = [], []
    for _ in range(TOP_K):
        mx = jnp.max(lg, axis=0, keepdims=True)
        ik = jnp.min(jnp.where(lg == mx, eidx, float(n_exp)), axis=0, keepdims=True)
        vals.append(mx)
        ids.append(ik)
        lg = jnp.where(eidx == ik, -jnp.inf, lg)
    e = [jnp.exp(v - vals[0]) for v in vals]
    tot = e[0] + e[1] + e[2] + e[3]
    ti_ref[...] = jnp.concatenate(ids, axis=0).astype(I32)
    tg_ref[...] = jnp.concatenate(e, axis=0) / tot
    tm = lg.shape[1]
    member = sum((eidx == ik).astype(F32) for ik in ids)
    earlier = (lax.broadcasted_iota(I32, (tm, tm), 0) < lax.broadcasted_iota(I32, (tm, tm), 1)).astype(BF16)
    seen = cnt_ref[:, 0:1] + _mm(member.astype(BF16), earlier)
    rk_ref[...] = jnp.concatenate(
        [jnp.sum(jnp.where(eidx == ik, seen, 0.0), axis=0, keepdims=True) for ik in ids], axis=0).astype(I32)
    cnt_ref[...] = cnt_ref[...] + jnp.sum(member, axis=1, keepdims=True)


def _outproj(l, x, hm, hd, mod, per_row, rows_per_batch, norm_g, w_out, rw_hi, rw_lo, rb, counts):
    t, d = x.shape
    tm = ROW_TILE
    rspec = lambda w: pl.BlockSpec((tm, w), lambda i: (i, 0))
    wspec = lambda a: pl.BlockSpec((None,) + a.shape[1:], lambda i: (l,) + (0,) * (a.ndim - 1))
    tspec = pl.BlockSpec((TOP_K, tm), lambda i: (0, i))
    cspec = pl.BlockSpec(counts.shape, lambda i: (0, 0))
    return pl.pallas_call(
        _outproj_kernel,
        grid=(t // tm,),
        in_specs=[rspec(d), rspec(GROUP_W), rspec(GROUP_W), _mod_spec(per_row, tm, d, rows_per_batch),
                  wspec(norm_g), wspec(w_out), wspec(rw_hi), wspec(rw_lo), wspec(rb), cspec],
        out_specs=[rspec(d), pl.BlockSpec((tm * ROW_CHUNKS, 128), lambda i: (i, 0)), tspec, tspec, tspec, cspec],
        out_shape=[jax.ShapeDtypeStruct((t, d), F32), jax.ShapeDtypeStruct((t * ROW_CHUNKS, 128), F32),
                   jax.ShapeDtypeStruct((TOP_K, t), I32), jax.ShapeDtypeStruct((TOP_K, t), F32),
                   jax.ShapeDtypeStruct((TOP_K, t), I32), jax.ShapeDtypeStruct(counts.shape, F32)],
        compiler_params=_cparams(1),
        name="outproj_norm_router",
    )(x, hm, hd, mod, norm_g, w_out, rw_hi, rw_lo, rb, counts)


def _route(top_i, rank, counts, tm, n_exp):
    t = top_i.shape[1]
    n_assign = t * TOP_K
    padded = (counts + tm - 1) // tm * tm
    pad_end = jnp.cumsum(padded)
    pad_start = pad_end - padded
    hit = top_i[None] == jnp.arange(n_exp, dtype=I32)[:, None, None]
    start = jnp.sum(jnp.where(hit, pad_start[:, None, None], 0), axis=0)
    pos = (start + rank).T.reshape(-1).astype(I32)
    n_blocks = -(-(n_assign + n_exp * (tm - 1)) // tm)
    slot_t = jnp.zeros((n_blocks * tm,), I32).at[pos].set(
        jnp.arange(n_assign, dtype=I32) // TOP_K, unique_indices=True)
    block_start = jnp.arange(n_blocks, dtype=I32) * tm
    block_e = jnp.minimum(jnp.sum((pad_end[None, :] <= block_start[:, None]).astype(I32), axis=1),
                          n_exp - 1).astype(I32)
    n_used = (pad_end[-1:] // tm).astype(I32)
    return pos, slot_t, block_e, n_used, n_blocks


def _experts_kernel(slot_ref, be_ref, nu_ref, x_hbm, wgu_ref, wd_ref, bgu_ref, bd_ref,
                    y_ref, xbuf, xbf, wgu_bf, wd_tmp, wd_bf, sem, *, tm):
    i = pl.program_id(0)
    n_used = nu_ref[0]
    slot = i % GATHER_SLOTS

    new_expert = (i == 0) | (be_ref[i] != be_ref[jnp.maximum(i - 1, 0)])

    @pl.when(new_expert & (i < n_used))
    def _():
        wgu_bf[...] = wgu_ref[...].astype(BF16)
        half_f = wd_ref.shape[0] // 2
        for c in range(wd_ref.shape[1] // 128):
            cs = slice(c * 128, (c + 1) * 128)
            wd_tmp[c, pl.ds(0, half_f, stride=2), :] = wd_ref[0:half_f, cs]
            wd_tmp[c, pl.ds(1, half_f, stride=2), :] = wd_ref[half_f:, cs]
            wd_bf[:, cs] = wd_tmp[c].astype(BF16)

    def gather(blk, dst_slot):
        base = blk * tm
        for r in range(tm):
            src = pl.multiple_of(slot_ref[base + r], ROW_CHUNKS)
            pltpu.make_async_copy(x_hbm.at[pl.ds(src, ROW_CHUNKS), :],
                                  xbuf.at[dst_slot, pl.ds(r * ROW_CHUNKS, ROW_CHUNKS), :], sem.at[dst_slot]).start()

    def block(prefetch):
        pltpu.make_async_copy(x_hbm.at[pl.ds(0, tm * ROW_CHUNKS), :], xbuf.at[slot], sem.at[slot]).wait()
        xbf[...] = _from_row_tiles(xbuf.at[slot]).astype(BF16)
        if prefetch:
            gather(i + GATHER_SLOTS - 1, (i + GATHER_SLOTS - 1) % GATHER_SLOTS)
        gu = _mm(xbf[...], wgu_bf[...]) + bgu_ref[...]
        half = gu.shape[1] // 2
        lo, hi = gu[:, :half], gu[:, half:]
        even = (lax.broadcasted_iota(I32, (1, half), 1) & 1) == 0
        g = jnp.where(even, lo, pltpu.roll(hi, 1, 1))
        u = jnp.where(even, pltpu.roll(lo, half - 1, 1), hi)
        g = jnp.minimum(g, SWIGLU_LIMIT)
        u = jnp.clip(u, -SWIGLU_LIMIT, SWIGLU_LIMIT)
        act = (u + 1.0) * (g * jax.nn.sigmoid(SWIGLU_ALPHA * g))
        _to_row_tiles(y_ref, _mm(act.astype(BF16), wd_bf[...]) + bd_ref[...])

    for ahead in range(GATHER_SLOTS - 1):
        @pl.when((i == 0) & (ahead < n_used))
        def _():
            gather(ahead, ahead)

    @pl.when(i + GATHER_SLOTS - 1 < n_used)
    def _():
        block(True)

    @pl.when((i < n_used) & (i + GATHER_SLOTS - 1 >= n_used))
    def _():
        block(False)

    @pl.when(i >= n_used)
    def _():
        y_ref[...] = jnp.zeros(y_ref.shape, F32)


def _experts(l, slot_src, block_e, n_used, n_blocks, h2, wgu, wd, bgu, bd):
    tm = ROW_TILE
    wspec = lambda a: pl.BlockSpec((None, None) + a.shape[2:], lambda i, st, be, nu: (l, be[i]) + (0,) * (a.ndim - 2))
    grid_spec = pltpu.PrefetchScalarGridSpec(
        num_scalar_prefetch=3,
        grid=(n_blocks,),
        in_specs=[pl.BlockSpec(memory_space=pl.ANY), wspec(wgu), wspec(wd), wspec(bgu), wspec(bd)],
        out_specs=pl.BlockSpec((tm * ROW_CHUNKS, 128), lambda i, st, be, nu: (i, 0)),
        scratch_shapes=[pltpu.VMEM((GATHER_SLOTS, tm * ROW_CHUNKS, 128), F32),
                        pltpu.VMEM((tm, ROW_CHUNKS * 128), BF16),
                        pltpu.VMEM(wgu.shape[2:], BF16), pltpu.VMEM((wd.shape[3] // 128, wd.shape[2], 128), F32),
                        pltpu.VMEM(wd.shape[2:], BF16), pltpu.SemaphoreType.DMA((GATHER_SLOTS,))],
    )
    return pl.pallas_call(
        functools.partial(_experts_kernel, tm=tm),
        grid_spec=grid_spec,
        out_shape=jax.ShapeDtypeStruct((n_blocks * tm * ROW_CHUNKS, 128), F32),
        compiler_params=_cparams(1, vmem=EXPERTS_VMEM_LIMIT),
        name="experts",
    )(slot_src, block_e, n_used, h2, wgu, wd, bgu, bd)


def _combine_kernel(pos_ref, x_ref, mod_ref, gate_ref, y_hbm, fg_ref, o_ref, buf, sem, *, tm, final):
    i = pl.program_id(0)

    def gather(blk, slot):
        def body(r, carry):
            dst = pl.multiple_of(r * ROW_CHUNKS, ROW_CHUNKS)
            for k in range(TOP_K):
                src = pl.multiple_of(pos_ref[(blk * tm + r) * TOP_K + k], ROW_CHUNKS)
                pltpu.make_async_copy(y_hbm.at[pl.ds(src, ROW_CHUNKS), :],
                                      buf.at[slot, k, pl.ds(dst, ROW_CHUNKS), :], sem.at[slot]).start()
            return carry
        lax.fori_loop(0, tm, body, 0, unroll=4)

    @pl.when(i == 0)
    def _():
        gather(0, 0)

    slot = i % 2

    @pl.when(i + 1 < pl.num_programs(0))
    def _():
        gather(i + 1, 1 - slot)

    for k in range(TOP_K):
        pltpu.make_async_copy(y_hbm.at[pl.ds(0, tm * ROW_CHUNKS), :], buf.at[slot, k], sem.at[slot]).wait()
    gate = gate_ref[...]
    y = gate[:, 0:1] * _from_row_tiles(buf.at[slot, 0])
    for k in range(1, TOP_K):
        y = y + gate[:, k:k + 1] * _from_row_tiles(buf.at[slot, k])
    out = x_ref[...] + mod_ref[5] * y
    o_ref[...] = _rms(out, fg_ref[...]) if final else out


def _combine(pos, x, mod, per_row, rows_per_batch, gate, ys, final_g, final):
    t, d = x.shape
    tm = ROW_TILE
    grid_spec = pltpu.PrefetchScalarGridSpec(
        num_scalar_prefetch=1,
        grid=(t // tm,),
        in_specs=[pl.BlockSpec((tm, d), lambda i, p: (i, 0)),
                  _wrap_prefetch(_mod_spec(per_row, tm, d, rows_per_batch)),
                  pl.BlockSpec((tm, TOP_K), lambda i, p: (i, 0)),
                  pl.BlockSpec(memory_space=pl.ANY),
                  pl.BlockSpec((1, d), lambda i, p: (0, 0))],
        out_specs=pl.BlockSpec((tm, d), lambda i, p: (i, 0)),
        scratch_shapes=[pltpu.VMEM((2, TOP_K, tm * ROW_CHUNKS, 128), F32), pltpu.SemaphoreType.DMA((2,))],
    )
    return pl.pallas_call(
        functools.partial(_combine_kernel, tm=tm, final=final),
        grid_spec=grid_spec,
        out_shape=jax.ShapeDtypeStruct((t, d), F32),
        compiler_params=_cparams(1),
        name="combine",
    )(pos, x, mod, gate, ys, final_g)


def _wrap_prefetch(spec):
    return pl.BlockSpec(spec.block_shape, lambda i, p: spec.index_map(i))


def kernel(x_prompt, x_sample, c_prompt, c_sample, cache_k, cache_v, state_C, state_n, state_m, page_table, norm1_g, ada_w, ada_b, w_in, b_igate, b_fgate, mlstm_norm_g, lam_q1, lam_k1, lam_q2, lam_k2, diff_norm_g, w_out, norm2_g, router_w, router_b, w_gu, b_gu, w_down, b_down, final_norm_g):
    n_layers = ada_w.shape[0]
    bp, seq, d = x_prompt.shape
    bd, n_tok, _ = x_sample.shape
    n_exp = router_w.shape[-1]
    tp, ts = bp * seq, bd * n_tok
    nqk = N_HEADS * HEAD_QK
    assert d == 2 * GROUP_W and n_tok == 8 and seq % MLSTM_CHUNK == 0 and ts % ROW_TILE == 0
    assert w_in.shape[-1] == 2 * nqk + GROUP_W + 2 * N_HEADS + 4 * GROUP_W

    g_lo = 2 * nqk + GROUP_W
    wa = w_in[:, :, :g_lo].astype(BF16)
    wb = w_in[:, :, g_lo + 2 * N_HEADS:].astype(BF16)
    w_gate = w_in[:, :, g_lo:g_lo + 2 * N_HEADS]
    wg_col = jnp.pad(w_gate, ((0, 0), (0, 0), (0, 128 - 2 * N_HEADS))).astype(BF16)
    wg_row = jnp.swapaxes(w_gate, 1, 2).astype(BF16)
    b_gate = jnp.concatenate([b_igate, b_fgate], axis=1).astype(F32)
    bg_col = jnp.pad(b_gate, ((0, 0), (0, 128 - 2 * N_HEADS)))[:, None, :]
    bg_row = b_gate[:, :, None]
    w_out_bf = w_out.astype(BF16)
    rw_t = jnp.swapaxes(router_w, 1, 2).astype(F32)
    rw_hi = rw_t.astype(BF16)
    rw_lo = (rw_t - rw_hi.astype(F32)).astype(BF16)
    rb = router_b.astype(F32)[:, :, None]
    b_gu_f = b_gu[..., None, :].astype(F32)
    b_d = b_down[..., None, :].astype(F32)
    n1g, n2g = norm1_g[:, None, :], norm2_g[:, None, :]
    mng, dng = mlstm_norm_g[:, None, :], diff_norm_g[:, None, :]
    fng = final_norm_g[None, :]
    slopes = jnp.asarray(2.0 ** (-8.0 * np.arange(1, N_HEADS + 1) / N_HEADS), F32)
    assert cache_k.shape[3:] == (N_HEADS, HEAD_V) and cache_v.shape[3:] == (N_HEADS, HEAD_V)
    pool, page = cache_k.shape[1], cache_k.shape[2]
    ck = cache_k.reshape(n_layers, pool, page * N_HEADS, HEAD_V)
    cv = cache_v.reshape(n_layers, pool, page * N_HEADS, HEAD_V)
    cinfo = _paged_row_info(n_tok)

    mod_all = _adaln_mod(jnp.concatenate([c_prompt, c_sample], axis=0), ada_w, ada_b)

    xp = x_prompt.reshape(tp, d)
    xs = x_sample.reshape(ts, d)
    zero_state = _pad_state(jnp.zeros((bp, N_HEADS, HEAD_V, HEAD_QK), F32), jnp.zeros((bp, N_HEADS, HEAD_QK), F32),
                            jnp.zeros((bp, N_HEADS), F32))
    state_s = [a.reshape((n_layers, bd) + a.shape[1:]) for a in _pad_state(
        state_C.reshape((n_layers * bd,) + state_C.shape[2:]), state_n.reshape((n_layers * bd,) + state_n.shape[2:]),
        state_m.reshape((n_layers * bd,) + state_m.shape[2:]))]
    n_chunks = seq // MLSTM_CHUNK
    no_counts = jnp.zeros((n_exp, 128), F32)
    outs = []
    for l in range(n_layers):
        lam_init = 0.8 - 0.6 * math.exp(-0.3 * l)
        lam = (jnp.exp(jnp.sum(lam_q1[l].astype(F32) * lam_k1[l].astype(F32)))
               - jnp.exp(jnp.sum(lam_q2[l].astype(F32) * lam_k2[l].astype(F32))) + lam_init)
        scalars = jnp.concatenate([lam[None], slopes]).astype(F32)
        mod_p = mod_all[l, :bp].reshape(bp, 6, 1, d)
        mod_s = jnp.repeat(mod_all[l, bp:].reshape(bd, 6, d).transpose(1, 0, 2), n_tok, axis=1)
        proj_w = (n1g, wa, wb, wg_col, wg_row, bg_col, bg_row)

        qk, vm, og, qd, kd_p, vd_p, gcol, grow = _inproj(l, xp, mod_p, False, seq, *proj_w)
        grow = grow.reshape(tp // ROW_TILE, 2 * N_HEADS, ROW_TILE // MLSTM_CHUNK, MLSTM_CHUNK)
        grow = grow.transpose(0, 2, 1, 3).reshape(tp // MLSTM_CHUNK, 2 * N_HEADS, MLSTM_CHUNK)
        hm_p, c_p, n_p, m_p = _mlstm(l, qk, vm, og, gcol, grow, *zero_state, mng, bp, n_chunks, MLSTM_CHUNK)
        hd_p = _attn_prompt(l, scalars, qd, kd_p, vd_p, dng, bp, seq, 1.0 - lam_init)

        qk, vm, og, qd, kd_s, vd_s, gcol, grow = _inproj(l, xs, mod_s, True, 0, *proj_w)
        pad_rows = lambda a: jnp.pad(a.reshape(bd, n_tok, a.shape[-1]),
                                     ((0, 0), (0, MLSTM_PAD - n_tok), (0, 0)))
        gcol = gcol.reshape(bd, n_tok, 128)
        gcol = jnp.concatenate(
            [gcol, jnp.broadcast_to(jnp.where(jnp.arange(128) < N_HEADS, NEG_BIG, 0.0).astype(F32),
                                    (bd, MLSTM_PAD - n_tok, 128))], axis=1).reshape(bd * MLSTM_PAD, 128)
        grow = grow.transpose(1, 0, 2).reshape(2 * N_HEADS, bd, n_tok).transpose(1, 0, 2)
        grow = jnp.concatenate(
            [grow, jnp.broadcast_to(jnp.where(jnp.arange(2 * N_HEADS) < N_HEADS, NEG_BIG, 0.0).astype(F32)[None, :, None],
                                    (bd, 2 * N_HEADS, MLSTM_PAD - n_tok))], axis=2)
        flat = lambda a: pad_rows(a).reshape(bd * MLSTM_PAD, a.shape[-1])
        hm_s, c_s, n_s, m_s = _mlstm(l, flat(qk), flat(vm), flat(og), gcol, grow,
                                     *[a[l] for a in state_s], mng, bd, 1, MLSTM_PAD)
        hm_s = hm_s.reshape(bd, MLSTM_PAD, GROUP_W)[:, :n_tok].reshape(ts, GROUP_W)
        wt = _paged_query_matrix(qd, bd, n_tok)
        hd_s = _attn_paged(l, page_table, scalars, wt, ck, cv, pad_rows(kd_s), pad_rows(vd_s), cinfo, dng,
                           1.0 - lam_init)
        hd_s = hd_s.reshape(ts, GROUP_W).astype(BF16)

        post_w = (n2g, w_out_bf, rw_hi, rw_lo, rb)
        xp, h2_p, ti_p, tg_p, rk_p, counts = _outproj(l, xp, hm_p, hd_p, mod_p, False, seq, *post_w, no_counts)
        xs, h2_s, ti_s, tg_s, rk_s, counts = _outproj(l, xs, hm_s, hd_s, mod_s, True, 0, *post_w, counts)
        h2 = jnp.concatenate([h2_p, h2_s], axis=0)
        pos, slot_t, block_e, n_used, n_blocks = _route(
            jnp.concatenate([ti_p, ti_s], axis=1), jnp.concatenate([rk_p, rk_s], axis=1),
            counts[:, 0].astype(I32), ROW_TILE, n_exp)
        ys = _experts(l, slot_t * ROW_CHUNKS, block_e, n_used, n_blocks, h2, w_gu, w_down, b_gu_f, b_d)
        final = l == n_layers - 1
        pos = pos * ROW_CHUNKS
        xp = _combine(pos[:tp * TOP_K], xp, mod_p, False, seq, tg_p.T, ys, fng, final)
        xs = _combine(pos[tp * TOP_K:], xs, mod_s, True, 0, tg_s.T, ys, fng, final)

        outs.append((kd_p, vd_p, c_p, n_p, m_p, kd_s, vd_s, c_s, n_s, m_s))

    def stacked(i, shape, dtype):
        return jnp.stack([o[i] for o in outs]).reshape((n_layers,) + shape).astype(dtype)

    sdt = state_C.dtype

    def final_states(first, n_seq):
        flat = _unpad_state(*[jnp.concatenate([o[first + i] for o in outs], axis=0) for i in range(3)])
        return [a.reshape((n_layers, n_seq) + a.shape[1:]).astype(sdt) for a in flat]

    states_p, states_s = final_states(2, bp), final_states(7, bd)
    st = lambda states, i: states[i]
    return (xp.reshape(bp, seq, d), xs.reshape(bd, n_tok, d),
            stacked(0, (bp, seq, N_HEADS, 2 * HEAD_QK), cache_k.dtype),
            stacked(1, (bp, seq, N_HEADS, HEAD_V), cache_v.dtype),
            st(states_p, 0), st(states_p, 1), st(states_p, 2),
            stacked(5, (bd, n_tok, N_HEADS, 2 * HEAD_QK), cache_k.dtype),
            stacked(6, (bd, n_tok, N_HEADS, HEAD_V), cache_v.dtype),
            st(states_s, 0), st(states_s, 1), st(states_s, 2))
```
